```python
import math
import jax, jax.numpy as jnp
from jax import lax
import numpy as np

D_MODEL = 1024
BATCH = 4
SEQ = 4096
DEPTH = 4
DEC_BATCH = 128
DEC_SEQ = 4
PAST_LEN = 2048
PAGE_SIZE = 128

BRANCH_W = 512
N_BRANCH = 4
A_HEADS = 8
A_DH = 64
MOBA_BLOCK = 256
MOBA_TOPK = 3
Q_BLOCK = 64
ROPE_DIM = A_DH // 4
ROPE_THETA = 500000.0
POOL_WINDOWS = (2, 4, 8, 16)
POOL_GROUPS = 4
POOL_GC = BRANCH_W // POOL_GROUPS
POOL_BUF = max(POOL_WINDOWS) - 1
C_HEADS = 4
C_DK = BRANCH_W // C_HEADS
C_DV = C_DK
HGRN_CHUNK = 64
SCONV_W = 3
D_FF = 2816
FFN_CONV_W = 3
DN_ALPHA = (2 * DEPTH) ** 0.25
DN_BETA = (8 * DEPTH) ** -0.25
LN_EPS = 1e-5
RMS_EPS = 1e-6

OFF_AQ = 0
OFF_AK = OFF_AQ + BRANCH_W
OFF_AV = OFF_AK + BRANCH_W
OFF_PU = OFF_AV + BRANCH_W
OFF_CQ = OFF_PU + BRANCH_W
OFF_CF = OFF_CQ + BRANCH_W
OFF_CI = OFF_CF + BRANCH_W
OFF_CG = OFF_CI + BRANCH_W
OFF_DB = OFF_CG + BRANCH_W
OFF_DC = OFF_DB + BRANCH_W
OFF_DH = OFF_DC + BRANCH_W
OFF_GATE = OFF_DH + BRANCH_W
N_IN = OFF_GATE + N_BRANCH * D_MODEL

kernel_name = "moba_pool_hgrn2_shortconv_gated_hybrid_step"

F32 = jnp.float32


def layer_norm(x, g, b):
    xf = x.astype(F32)
    mu = xf.mean(-1, keepdims=True)
    var = ((xf - mu) ** 2).mean(-1, keepdims=True)
    return ((xf - mu) * lax.rsqrt(var + LN_EPS) * g + b).astype(x.dtype)


def partial_rope(x, pos):
    half = ROPE_DIM // 2
    inv = jnp.power(ROPE_THETA, -jnp.arange(0, ROPE_DIM, 2, dtype=F32) / ROPE_DIM)
    ang = pos.astype(F32)[:, None] * inv[None, :]
    cos = jnp.cos(ang)[None, :, None, :]
    sin = jnp.sin(ang)[None, :, None, :]
    xf = x.astype(F32)
    x1 = xf[..., :half]
    x2 = xf[..., half:ROPE_DIM]
    out = jnp.concatenate([x1 * cos - x2 * sin, x2 * cos + x1 * sin, xf[..., ROPE_DIM:]], -1)
    return out.astype(x.dtype)


def moba_attention(q, k, v, pos0):
    B, Tq, H, dh = q.shape
    L = k.shape[1]
    nb = -(-L // MOBA_BLOCK)
    pad = nb * MOBA_BLOCK - L
    k = jnp.pad(k, ((0, 0), (0, pad), (0, 0), (0, 0)))
    v = jnp.pad(v, ((0, 0), (0, pad), (0, 0), (0, 0)))
    kb = k.reshape(B, nb, MOBA_BLOCK, H, dh).transpose(0, 3, 1, 2, 4)
    vb = v.reshape(B, nb, MOBA_BLOCK, H, dh).transpose(0, 3, 1, 2, 4)
    kmean = kb.astype(F32).mean(axis=3)
    kk = min(MOBA_TOPK, nb)
    qb = math.gcd(Tq, Q_BLOCK)
    nq = Tq // qb
    qh = q.transpose(0, 2, 1, 3).reshape(B, H, nq, qb, dh).transpose(2, 0, 1, 3, 4)
    scale = dh ** -0.5
    bi = jnp.arange(B)[:, None, None]
    hi = jnp.arange(H)[None, :, None]
    key_in_blk = jnp.arange(MOBA_BLOCK)

    def one_block(args):
        qblk, start = args
        pos = pos0 + start + jnp.arange(qb)
        own = pos // MOBA_BLOCK
        gate = jnp.einsum('bhqd,bhnd->bhqn', qblk.astype(F32), kmean)
        past_ok = jnp.arange(nb)[None, :] < own[:, None]
        gate = jnp.where(past_ok, gate, -jnp.inf)
        _, top_idx = lax.top_k(gate, kk)
        top_ok = jnp.arange(kk)[None, :] < own[:, None]
        sel = jnp.concatenate([top_idx, jnp.broadcast_to(own, (B, H, qb))[..., None]], -1)
        own_ok = (own[:, None] * MOBA_BLOCK + key_in_blk[None, :]) <= pos[:, None]
        mask = jnp.concatenate([jnp.broadcast_to(top_ok[:, :, None], (qb, kk, MOBA_BLOCK)),
                                own_ok[:, None, :]], axis=1)
        scores = jnp.stack([
            jnp.einsum('bhqd,bhqnd->bhqn', qblk, kb[bi, hi, sel[..., s]]).astype(F32)
            for s in range(kk + 1)], axis=3) * scale
        scores = jnp.where(mask, scores, -jnp.inf)
        p = jax.nn.softmax(scores.reshape(B, H, qb, -1), axis=-1).reshape(scores.shape)
        out = jnp.zeros((B, H, qb, dh), F32)
        for s in range(kk + 1):
            out = out + jnp.einsum('bhqn,bhqnd->bhqd', p[:, :, :, s], vb[bi, hi, sel[..., s]].astype(F32))
        return out.astype(q.dtype)

    out = lax.map(one_block, (qh, jnp.arange(nq) * qb))
    return out.transpose(1, 0, 3, 2, 4).reshape(B, Tq, H * dh)


def pool_mixer(u, buf, pos, w_grp, scale):
    B, T, _ = u.shape
    full = jnp.concatenate([buf.astype(u.dtype), u], axis=1)
    cs = jnp.cumsum(full.astype(F32), axis=1)
    cs = jnp.concatenate([jnp.zeros((B, 1, cs.shape[-1]), F32), cs], axis=1)
    uf = u.astype(F32)
    outs = []
    for g, w in enumerate(POOL_WINDOWS):
        sl = slice(g * POOL_GC, (g + 1) * POOL_GC)
        win = cs[:, POOL_BUF + 1:POOL_BUF + 1 + T, sl] - cs[:, POOL_BUF + 1 - w:POOL_BUF + 1 - w + T, sl]
        cnt = jnp.minimum(w, pos + 1).astype(F32)[None, :, None]
        outs.append(win / cnt - uf[..., sl])
    pooled = jnp.stack(outs, axis=2)
    y = jnp.einsum('btgc,gce->btge', pooled, w_grp.astype(F32)).reshape(B, T, BRANCH_W) * scale
    return y, full[:, T:]


def gla_chunked(q, logf, k, v, S0):
    B, T, H, dk = q.shape
    dv = v.shape[-1]
    C = math.gcd(T, HGRN_CHUNK)
    n = T // C

    def to_chunks(a):
        return a.reshape(B, n, C, H, a.shape[-1]).transpose(1, 0, 3, 2, 4)

    causal = jnp.tril(jnp.ones((C, C), bool))

    def step(S, inp):
        qc, gc, kc, vc = inp
        b = jnp.cumsum(gc, axis=2)
        diff = jnp.where(causal[:, :, None], b[:, :, :, None, :] - b[:, :, None, :, :], -jnp.inf)
        attn = jnp.einsum('bhtk,bhtsk,bhsk->bhts', qc, jnp.exp(diff), kc)
        o = jnp.einsum('bhts,bhsv->bhtv', attn, vc) + jnp.einsum('bhtk,bhkv->bhtv', qc * jnp.exp(b), S)
        b_last = b[:, :, -1:, :]
        S = jnp.exp(b_last[:, :, 0])[..., None] * S + jnp.einsum('bhsk,bhsv->bhkv', kc * jnp.exp(b_last - b), vc)
        return S, o

    S, o = lax.scan(step, S0, (to_chunks(q), to_chunks(logf), to_chunks(k), to_chunks(v)))
    return o.transpose(1, 0, 3, 2, 4).reshape(B, T, H, dv), S


def hgrn2_mixer(q_raw, f_raw, i_raw, g_raw, S0, lb, norm_g):
    B, T, _ = q_raw.shape

    def heads(a):
        return a.astype(F32).reshape(B, T, C_HEADS, -1)

    q = jax.nn.silu(heads(q_raw))
    f = lb + (1.0 - lb) * jax.nn.sigmoid(f_raw.astype(F32))
    o, S = gla_chunked(q, heads(jnp.log(f)), heads(1.0 - f), heads(i_raw), S0.astype(F32))
    o = o * lax.rsqrt((o * o).mean(-1, keepdims=True) + RMS_EPS)
    o = o.reshape(B, T, BRANCH_W) * norm_g * jax.nn.silu(g_raw.astype(F32))
    return o, S


def causal_dwconv(u, buf, w):
    width = w.shape[0]
    T = u.shape[1]
    full = jnp.concatenate([buf.astype(u.dtype), u], axis=1)
    y = sum(full[:, j:j + T] * w[j] for j in range(width))
    return y, full[:, T:]


def decoder_layer(x, pos0, past_k, past_v, pool_buf, hgrn_s, sconv_buf, ffn_buf, lb,
                  w_in, w_branch, w_o, pool_w, pool_scale, hgrn_norm_g, sconv_w,
                  ln1_g, ln1_b, ffn_up, ffn_conv, ffn_down, ln2_g, ln2_b):
    B, T, _ = x.shape
    pos = pos0 + jnp.arange(T)
    proj = x @ w_in

    def col(o):
        return proj[..., o:o + BRANCH_W]

    q = partial_rope(col(OFF_AQ).reshape(B, T, A_HEADS, A_DH), pos)
    k = partial_rope(col(OFF_AK).reshape(B, T, A_HEADS, A_DH), pos)
    v = col(OFF_AV).reshape(B, T, A_HEADS, A_DH)
    if past_k is None:
        k_all, v_all = k, v
    else:
        k_all = jnp.concatenate([past_k.astype(k.dtype), k], axis=1)
        v_all = jnp.concatenate([past_v.astype(v.dtype), v], axis=1)
    y_a = moba_attention(q, k_all, v_all, pos0)
    y_b, new_pool = pool_mixer(col(OFF_PU), pool_buf, pos, pool_w, pool_scale)
    y_c, new_s = hgrn2_mixer(col(OFF_CQ), col(OFF_CF), col(OFF_CI), col(OFF_CG), hgrn_s, lb, hgrn_norm_g)
    conv_out, new_sconv = causal_dwconv(col(OFF_DC) * col(OFF_DH), sconv_buf, sconv_w)
    y_d = col(OFF_DB) * conv_out
    ys = jnp.stack([y_a.astype(x.dtype), y_b.astype(x.dtype), y_c.astype(x.dtype), y_d.astype(x.dtype)], axis=2)
    gates = jax.nn.sigmoid(proj[..., OFF_GATE:].reshape(B, T, N_BRANCH, D_MODEL))
    branches = jnp.einsum('btnc,ncd->btnd', ys, w_branch)
    mixed = (gates * branches).sum(axis=2) @ w_o
    h = layer_norm(DN_ALPHA * x + mixed, ln1_g, ln1_b)
    up, new_ffn = causal_dwconv(h @ ffn_up, ffn_buf, ffn_conv)
    gate, val = up[..., :D_FF], up[..., D_FF:]
    f = (jax.nn.silu(gate) * val) @ ffn_down
    out = layer_norm(DN_ALPHA * h + f, ln2_g, ln2_b)
    return out, k, v, new_pool, new_s, new_sconv, new_ffn


def setup_inputs(seed: int = 0) -> dict:
    key = jax.random.key(seed)
    ks = jax.random.split(key, 32)
    nrm = jax.random.normal
    n_pages = PAST_LEN // PAGE_SIZE
    n_pool = (DEC_BATCH * n_pages * 5) // 4
    perm = jax.random.permutation(ks[0], n_pool)
    page_table = perm[:DEC_BATCH * n_pages].reshape(DEC_BATCH, n_pages).astype(jnp.int32)
    return {
        "x_prompt": nrm(ks[1], (BATCH, SEQ, D_MODEL), F32),
        "x_sample": nrm(ks[2], (DEC_BATCH, DEC_SEQ, D_MODEL), F32),
        "cache_k": nrm(ks[3], (DEPTH, n_pool, PAGE_SIZE, A_HEADS, A_DH), F32),
        "cache_v": nrm(ks[4], (DEPTH, n_pool, PAGE_SIZE, A_HEADS, A_DH), F32),
        "state_pool": nrm(ks[5], (DEPTH, DEC_BATCH, POOL_BUF, BRANCH_W), F32),
        "state_hgrn": 0.3 * nrm(ks[6], (DEPTH, DEC_BATCH, C_HEADS, C_DK, C_DV), F32),
        "state_sconv": nrm(ks[7], (DEPTH, DEC_BATCH, SCONV_W - 1, BRANCH_W), F32),
        "state_ffn": nrm(ks[8], (DEPTH, DEC_BATCH, FFN_CONV_W - 1, 2 * D_FF), F32),
        "page_table": page_table,
        "w_in": nrm(ks[9], (DEPTH, D_MODEL, N_IN), F32) * D_MODEL ** -0.5,
        "w_branch": nrm(ks[10], (DEPTH, N_BRANCH, BRANCH_W, D_MODEL), F32) * (BRANCH_W ** -0.5 * DN_BETA),
        "w_o": nrm(ks[11], (DEPTH, D_MODEL, D_MODEL), F32) * (D_MODEL ** -0.5 * DN_BETA),
        "pool_w": nrm(ks[12], (DEPTH, POOL_GROUPS, POOL_GC, POOL_GC), F32) * POOL_GC ** -0.5,
        "pool_scale": 1.0 + 0.1 * nrm(ks[13], (DEPTH, BRANCH_W), F32),
        "hgrn_lb_logits": 0.5 * nrm(ks[14], (DEPTH, BRANCH_W), F32),
        "hgrn_norm_g": 1.0 + 0.1 * nrm(ks[15], (DEPTH, BRANCH_W), F32),
        "sconv_w": nrm(ks[16], (DEPTH, SCONV_W, BRANCH_W), F32) * SCONV_W ** -0.5,
        "ln1_g": 1.0 + 0.02 * nrm(ks[17], (DEPTH, D_MODEL), F32),
        "ln1_b": 0.02 * nrm(ks[18], (DEPTH, D_MODEL), F32),
        "ffn_up": nrm(ks[19], (DEPTH, D_MODEL, 2 * D_FF), F32) * D_MODEL ** -0.5,
        "ffn_conv": nrm(ks[20], (DEPTH, FFN_CONV_W, 2 * D_FF), F32) * FFN_CONV_W ** -0.5,
        "ffn_down": nrm(ks[21], (DEPTH, D_FF, D_MODEL), F32) * (D_FF ** -0.5 * DN_BETA),
        "ln2_g": 1.0 + 0.02 * nrm(ks[22], (DEPTH, D_MODEL), F32),
        "ln2_b": 0.02 * nrm(ks[23], (DEPTH, D_MODEL), F32),
    }


def reference(x_prompt, x_sample, cache_k, cache_v, state_pool, state_hgrn, state_sconv, state_ffn,
              page_table, w_in, w_branch, w_o, pool_w, pool_scale, hgrn_lb_logits, hgrn_norm_g,
              sconv_w, ln1_g, ln1_b, ffn_up, ffn_conv, ffn_down, ln2_g, ln2_b):
    p_lb = jax.nn.softmax(hgrn_lb_logits.astype(F32), axis=0)
    lb_all = jnp.cumsum(p_lb, axis=0) - p_lb[0:1]
    n_pages = PAST_LEN // PAGE_SIZE
    Bp = x_prompt.shape[0]
    Bd = x_sample.shape[0]
    dt = x_prompt.dtype
    hp, hs = x_prompt, x_sample
    kp_l, vp_l, poolp_l, sp_l, scp_l, fp_l = [], [], [], [], [], []
    ks_l, vs_l, pools_l, ss_l, scs_l, fs_l = [], [], [], [], [], []
    for l in range(DEPTH):
        wl = (w_in[l], w_branch[l], w_o[l], pool_w[l], pool_scale[l], hgrn_norm_g[l], sconv_w[l],
              ln1_g[l], ln1_b[l], ffn_up[l], ffn_conv[l], ffn_down[l], ln2_g[l], ln2_b[l])
        hp, kp, vp, poolp, sp, scp, fp = decoder_layer(
            hp, 0, None, None,
            jnp.zeros((Bp, POOL_BUF, BRANCH_W), dt),
            jnp.zeros((Bp, C_HEADS, C_DK, C_DV), F32),
            jnp.zeros((Bp, SCONV_W - 1, BRANCH_W), dt),
            jnp.zeros((Bp, FFN_CONV_W - 1, 2 * D_FF), dt),
            lb_all[l], *wl)
        past_k = cache_k[l][page_table].reshape(Bd, n_pages * PAGE_SIZE, A_HEADS, A_DH)
        past_v = cache_v[l][page_table].reshape(Bd, n_pages * PAGE_SIZE, A_HEADS, A_DH)
        hs, ks_, vs_, pools, ss, scs, fs = decoder_layer(
            hs, PAST_LEN, past_k, past_v, state_pool[l], state_hgrn[l], state_sconv[l], state_ffn[l],
            lb_all[l], *wl)
        kp_l.append(kp); vp_l.append(vp); poolp_l.append(poolp); sp_l.append(sp); scp_l.append(scp); fp_l.append(fp)
        ks_l.append(ks_); vs_l.append(vs_); pools_l.append(pools); ss_l.append(ss); scs_l.append(scs); fs_l.append(fs)
    return (hp, hs,
            jnp.stack(kp_l), jnp.stack(vp_l), jnp.stack(poolp_l), jnp.stack(sp_l), jnp.stack(scp_l), jnp.stack(fp_l),
            jnp.stack(ks_l), jnp.stack(vs_l), jnp.stack(pools_l), jnp.stack(ss_l), jnp.stack(scs_l), jnp.stack(fs_l))
```

```python
import functools
import math

import jax
import jax.numpy as jnp
import numpy as np
from jax import lax
from jax.experimental import pallas as pl
from jax.experimental.pallas import tpu as pltpu

F32 = jnp.float32
BF = jnp.bfloat16
NEG_INF = float("-inf")

D_MODEL = 1024
BRANCH_W = 512
N_BRANCH = 4
A_HEADS = 8
A_DH = 64
MOBA_BLOCK = 256
MOBA_TOPK = 3
ROPE_DIM = A_DH // 4
ROPE_THETA = 500000.0
POOL_WINDOWS = (2, 4, 8, 16)
POOL_GC = 128
POOL_BUF = 15
C_HEADS = 4
C_DK = 128
SCONV_W = 3
D_FF = 2816
FFN_CONV_W = 3
PAGE_SIZE = 128
DEPTH = 4
DN_ALPHA = (2 * DEPTH) ** 0.25
LN_EPS = 1e-5
RMS_EPS = 1e-6

CB_AQ, CB_AK, CB_AV, CB_PU, CB_CQ, CB_CF, CB_CI, CB_CG, CB_DB, CB_DC, CB_DH, CB_GATE = range(12)
N_IN = (CB_GATE + 2 * N_BRANCH) * BRANCH_W

LANES = 128
HALO = 16
CHUNK = 16
VMEM_LIMIT = 56 * 1024 * 1024

NT = (((1,), (1,)), ((), ()))
TN = (((0,), (0,)), ((), ()))


def _cp(sem, vmem=VMEM_LIMIT):
    return pltpu.CompilerParams(dimension_semantics=sem, vmem_limit_bytes=vmem)


def _sigmoid(x):
    return 1.0 / (1.0 + jnp.exp(-x))


def _silu(x):
    return x * _sigmoid(x)


def _layer_norm(x, g, b):
    mu = jnp.mean(x, axis=-1, keepdims=True)
    xc = x - mu
    var = jnp.mean(xc * xc, axis=-1, keepdims=True)
    return xc * lax.rsqrt(var + LN_EPS) * g + b


def _lb_body(logit_ref, o_ref):
    x = logit_ref[...]
    m = jnp.max(x, axis=0, keepdims=True)
    e = jnp.exp(x - m)
    p = e / jnp.sum(e, axis=0, keepdims=True)
    acc = jnp.zeros_like(p[0:1])
    rows = []
    for l in range(x.shape[0]):
        acc = acc + p[l:l + 1]
        rows.append(acc - p[0:1])
    o_ref[...] = jnp.concatenate(rows, axis=0)


def _lower_bounds(logits):
    return pl.pallas_call(_lb_body, out_shape=jax.ShapeDtypeStruct(logits.shape, F32), name="hgrn_lb")(logits)


def _mm_body(x_ref, w_ref, o_ref, xb_ref):
    @pl.when(pl.program_id(1) == 0)
    def _():
        xb_ref[...] = x_ref[...].astype(BF)

    o_ref[...] = jnp.dot(xb_ref[...], w_ref[...], preferred_element_type=F32)


def _matmul(x, w_bf, tm, tn, name):
    M, K = x.shape
    N = w_bf.shape[1]
    assert M % tm == 0 and N % tn == 0
    return pl.pallas_call(
        _mm_body,
        grid=(M // tm, N // tn),
        in_specs=[pl.BlockSpec((tm, K), lambda i, j: (i, 0)),
                  pl.BlockSpec((K, tn), lambda i, j: (0, j))],
        out_specs=pl.BlockSpec((tm, tn), lambda i, j: (i, j)),
        out_shape=jax.ShapeDtypeStruct((M, N), F32),
        scratch_shapes=[pltpu.VMEM((tm, K), BF)],
        compiler_params=_cp(("parallel", "arbitrary")),
        name=name,
    )(x, w_bf)


def _rope_tables(pos):
    half = ROPE_DIM // 2
    inv = jnp.power(ROPE_THETA, -jnp.arange(0, ROPE_DIM, 2, dtype=F32) / ROPE_DIM)
    ang = pos.astype(F32)[:, None] * inv[None, :]
    cos, sin = jnp.cos(ang), jnp.sin(ang)
    n = pos.shape[0]
    ones = jnp.ones((n, A_DH - ROPE_DIM), F32)
    zeros = jnp.zeros((n, A_DH - ROPE_DIM), F32)
    zh = jnp.zeros((n, half), F32)
    c = jnp.concatenate([cos, cos, ones], axis=1)
    s_lo = jnp.concatenate([-sin, zh, zeros], axis=1)
    s_hi = jnp.concatenate([zh, sin, zeros], axis=1)
    tile = lambda a: jnp.tile(a, (1, A_HEADS))
    return tile(c), tile(s_lo), tile(s_hi)


def _rope_body(q_ref, k_ref, v_ref, c_ref, sl_ref, sh_ref, qf_ref, qb_ref, kf_ref, kb_ref, vb_ref):
    c, sl, sh = c_ref[...], sl_ref[...], sh_ref[...]
    half = ROPE_DIM // 2
    w = q_ref.shape[1]

    def rope(x):
        return x * c + pltpu.roll(x, w - half, 1) * sl + pltpu.roll(x, half, 1) * sh

    q = rope(q_ref[...])
    k = rope(k_ref[...])
    qf_ref[...] = q
    qb_ref[...] = (q * (A_DH ** -0.5)).astype(BF)
    kf_ref[...] = k
    kb_ref[...] = k.astype(BF)
    vb_ref[...] = v_ref[...].astype(BF)


def _rope(proj, tables, tm):
    M = proj.shape[0]
    ntab = tables[0].shape[0] // tm
    blk = lambda cb: pl.BlockSpec((tm, BRANCH_W), lambda i, cb=cb: (i, cb))
    tab = pl.BlockSpec((tm, BRANCH_W), lambda i: (i % ntab, 0))
    out = pl.BlockSpec((tm, BRANCH_W), lambda i: (i, 0))
    sds = lambda dt: jax.ShapeDtypeStruct((M, BRANCH_W), dt)
    return pl.pallas_call(
        _rope_body,
        grid=(M // tm,),
        in_specs=[blk(CB_AQ), blk(CB_AK), blk(CB_AV), tab, tab, tab],
        out_specs=[out] * 5,
        out_shape=[sds(F32), sds(BF), sds(F32), sds(BF), sds(BF)],
        compiler_params=_cp(("parallel",)),
        name="rope",
    )(proj, proj, proj, *tables)


def _top_blocks_bias(gate, n_past):
    col = lax.broadcasted_iota(jnp.int32, gate.shape, 1)
    g = jnp.where(col < n_past, gate, NEG_INF)
    bias = jnp.full(gate.shape, NEG_INF, F32)
    for _ in range(MOBA_TOPK):
        m = jnp.max(g, axis=1, keepdims=True)
        idx = jnp.min(jnp.where(g == m, col, LANES), axis=1, keepdims=True)
        pick = (col == idx) & (m > NEG_INF)
        bias = jnp.where(pick, 0.0, bias)
        g = jnp.where(pick, NEG_INF, g)
    return bias


def _column(x, j):
    col = lax.broadcasted_iota(jnp.int32, x.shape, 1)
    return jnp.max(jnp.where(col == j, x, NEG_INF), axis=1, keepdims=True)


def _kmean_body(k_ref, o_ref):
    nb = o_ref.shape[1]
    x = k_ref[...].reshape(nb, MOBA_BLOCK, BRANCH_W)
    o_ref[0] = jnp.sum(x, axis=1) * (1.0 / MOBA_BLOCK)


def _kmean(kf, B, T):
    nb = T // MOBA_BLOCK
    return pl.pallas_call(
        _kmean_body,
        grid=(B,),
        in_specs=[pl.BlockSpec((T, BRANCH_W), lambda b: (b, 0))],
        out_specs=pl.BlockSpec((1, nb, BRANCH_W), lambda b: (b, 0, 0)),
        out_shape=jax.ShapeDtypeStruct((B, nb, BRANCH_W), F32),
        compiler_params=_cp(("parallel",)),
        name="kmean",
    )(kf)


def _attn_body(qf_ref, qb_ref, k_ref, v_ref, km_ref, o_ref):
    i = pl.program_id(2)
    tq = MOBA_BLOCK
    qf, qb, km = qf_ref[...], qb_ref[...], km_ref[...]
    lane = lax.broadcasted_iota(jnp.int32, (tq, LANES), 1)
    causal = lax.broadcasted_iota(jnp.int32, (tq, tq), 0) >= lax.broadcasted_iota(jnp.int32, (tq, tq), 1)
    own = pl.ds(pl.multiple_of(i * tq, tq), tq)
    outs = []
    for e in range(LANES // A_DH):
        hm = (lane >= e * A_DH) & (lane < (e + 1) * A_DH)
        gate = lax.dot_general(jnp.where(hm, qf, 0.0), km, NT, precision=lax.Precision.HIGHEST,
                               preferred_element_type=F32)
        bias = _top_blocks_bias(gate, i)
        qe = jnp.where(hm, qb, jnp.zeros_like(qb))
        s = lax.dot_general(qe, k_ref[own, :], NT, preferred_element_type=F32)
        s = jnp.where(causal, s, NEG_INF)
        m0 = jnp.max(s, axis=1, keepdims=True)
        p = jnp.exp(s - m0)
        l0 = jnp.sum(p, axis=1, keepdims=True)
        acc0 = jnp.dot(p.astype(BF), v_ref[own, :], preferred_element_type=F32)

        def body(j, carry, qe=qe, bias=bias):
            m, l, acc = carry
            blk = pl.ds(pl.multiple_of(j * tq, tq), tq)
            s = lax.dot_general(qe, k_ref[blk, :], NT, preferred_element_type=F32) + _column(bias, j)
            mn = jnp.maximum(m, jnp.max(s, axis=1, keepdims=True))
            a = jnp.exp(m - mn)
            p = jnp.exp(s - mn)
            l = a * l + jnp.sum(p, axis=1, keepdims=True)
            acc = a * acc + jnp.dot(p.astype(BF), v_ref[blk, :], preferred_element_type=F32)
            return mn, l, acc

        _, l, acc = lax.fori_loop(0, i, body, (m0, l0, acc0))
        outs.append(acc / l)
    o_ref[...] = jnp.where(lane < A_DH, outs[0], outs[1]).astype(o_ref.dtype)


def _prompt_attention(qf, qb, kb, vb, kmp, B, T):
    nq = T // MOBA_BLOCK
    npair = BRANCH_W // LANES
    qspec = pl.BlockSpec((MOBA_BLOCK, LANES), lambda b, hp, i: (b * nq + i, hp))
    kvspec = pl.BlockSpec((T, LANES), lambda b, hp, i: (b, hp))
    return pl.pallas_call(
        _attn_body,
        grid=(B, npair, nq),
        in_specs=[qspec, qspec, kvspec, kvspec,
                  pl.BlockSpec((None, None, LANES, LANES), lambda b, hp, i: (b, hp, 0, 0))],
        out_specs=qspec,
        out_shape=jax.ShapeDtypeStruct((B * T, BRANCH_W), BF),
        compiler_params=_cp(("parallel", "parallel", "arbitrary")),
        name="moba_prompt",
    )(qf, qb, kb, vb, kmp)


def _pool_group(ext_ref, u, g, w, t_first, pw, ps):
    tm = u.shape[0]
    sl = slice(g * POOL_GC, (g + 1) * POOL_GC)
    win = ext_ref[HALO:HALO + tm, sl]
    for r in range(1, w):
        win = win + ext_ref[HALO - r:HALO - r + tm, sl]
    t = t_first + lax.broadcasted_iota(jnp.int32, (tm, POOL_GC), 0)
    cnt = jnp.minimum(w, t + 1).astype(F32)
    pooled = win / cnt - u[:, sl]
    return jnp.dot(pooled.astype(BF), pw, preferred_element_type=F32) * ps[:, sl]


def _mix_body(u_ref, uh_ref, db_ref, dc_ref, dh_ref, dch_ref, dhh_ref, pw_ref, ps_ref, sw_ref,
              yb_ref, yd_ref, zt_ref, ext_ref, zext_ref):
    i = pl.program_id(1)
    tm = u_ref.shape[0]
    first = i == 0
    u = u_ref[...]
    ext_ref[0:HALO, :] = jnp.where(first, 0.0, uh_ref[...])
    ext_ref[HALO:, :] = u
    ps = ps_ref[...]
    for g, w in enumerate(POOL_WINDOWS):
        y = _pool_group(ext_ref, u, g, w, i * tm, pw_ref[g], ps)
        yb_ref[:, g * POOL_GC:(g + 1) * POOL_GC] = y.astype(yb_ref.dtype)
    z = dc_ref[...] * dh_ref[...]
    zext_ref[0:8, :] = jnp.where(first, 0.0, dch_ref[...] * dhh_ref[...])
    zext_ref[8:, :] = z
    sw = sw_ref[...]
    conv = sw[0:1] * zext_ref[6:6 + tm, :] + sw[1:2] * zext_ref[7:7 + tm, :] + sw[2:3] * z
    yd_ref[...] = (db_ref[...] * conv).astype(yd_ref.dtype)
    zt_ref[0] = z[tm - 8:tm]


def _prompt_mixers(proj, pool_w_bf, pool_scale, sconv_w, B, T, tm):
    nt = T // tm
    row = lambda b, i: b * nt + i
    blk = lambda cb: pl.BlockSpec((tm, BRANCH_W), lambda b, i, cb=cb: (row(b, i), cb))

    def halo(cb, h):
        return pl.BlockSpec((h, BRANCH_W), lambda b, i, cb=cb, h=h: (jnp.maximum(row(b, i) * (tm // h) - 1, 0), cb))

    const = lambda shape: pl.BlockSpec(shape, lambda b, i, n=len(shape): (0,) * n)
    return pl.pallas_call(
        _mix_body,
        grid=(B, nt),
        in_specs=[blk(CB_PU), halo(CB_PU, HALO), blk(CB_DB), blk(CB_DC), blk(CB_DH), halo(CB_DC, 8), halo(CB_DH, 8),
                  const((len(POOL_WINDOWS), POOL_GC, POOL_GC)), const((1, BRANCH_W)), const((SCONV_W, BRANCH_W))],
        out_specs=[pl.BlockSpec((tm, BRANCH_W), lambda b, i: (row(b, i), 0)),
                   pl.BlockSpec((tm, BRANCH_W), lambda b, i: (row(b, i), 0)),
                   pl.BlockSpec((1, 8, BRANCH_W), lambda b, i: (b, 0, 0))],
        out_shape=[jax.ShapeDtypeStruct((B * T, BRANCH_W), BF), jax.ShapeDtypeStruct((B * T, BRANCH_W), BF),
                   jax.ShapeDtypeStruct((B, 8, BRANCH_W), F32)],
        scratch_shapes=[pltpu.VMEM((tm + HALO, BRANCH_W), F32), pltpu.VMEM((tm + 8, BRANCH_W), F32)],
        compiler_params=_cp(("parallel", "arbitrary")),
        name="pool_sconv_prompt",
    )(proj, proj, proj, proj, proj, proj, proj, pool_w_bf, pool_scale, sconv_w)


def _gla_chunk(cq, cf, ci, lb, st, n_valid):
    rows = cq.shape[0]
    f = lb + (1.0 - lb) * _sigmoid(cf)
    g = jnp.log(f)
    r_i = lax.broadcasted_iota(jnp.int32, (rows, rows), 0)
    c_i = lax.broadcasted_iota(jnp.int32, (rows, rows), 1)
    tri = jnp.where((r_i >= c_i) & (c_i < n_valid), 1.0, 0.0).astype(F32)
    b = jnp.dot(tri, g, precision=lax.Precision.HIGHEST, preferred_element_type=F32)
    qs = _silu(cq)
    kk = 1.0 - f
    t_i = lax.broadcasted_iota(jnp.int32, (rows, 1), 0)
    o = jnp.zeros((rows, C_DK), F32)
    for s in range(n_valid):
        e = jnp.exp(jnp.minimum(b - b[s:s + 1], 0.0))
        a = jnp.sum(qs * kk[s:s + 1] * e, axis=1, keepdims=True)
        o = o + jnp.where(t_i >= s, a, 0.0) * ci[s:s + 1]
    o = o + lax.dot_general((qs * jnp.exp(b)).astype(BF), st.astype(BF), NT, preferred_element_type=F32)
    bl = b[n_valid - 1:n_valid]
    ks = jnp.where(t_i < n_valid, kk * jnp.exp(jnp.minimum(bl - b, 0.0)), 0.0)
    upd = lax.dot_general(ci.astype(BF), ks.astype(BF), TN, preferred_element_type=F32)
    return o, st * jnp.exp(bl) + upd


def _hgrn_out(o, cg, ng):
    o = o * lax.rsqrt(jnp.mean(o * o, axis=1, keepdims=True) + RMS_EPS)
    return o * ng * _silu(cg)


def _hgrn_body(cq_ref, cf_ref, ci_ref, cg_ref, lb_ref, ng_ref, yc_ref, so_ref, st_ref):
    it = pl.program_id(2)
    tc = cq_ref.shape[0]

    @pl.when(it == 0)
    def _():
        st_ref[...] = jnp.zeros_like(st_ref)

    lb, ng = lb_ref[...], ng_ref[...]

    def chunk(c, carry):
        rows = pl.ds(pl.multiple_of(c * CHUNK, CHUNK), CHUNK)
        o, st = _gla_chunk(cq_ref[rows, :], cf_ref[rows, :], ci_ref[rows, :], lb, st_ref[...], CHUNK)
        st_ref[...] = st
        yc_ref[rows, :] = _hgrn_out(o, cg_ref[rows, :], ng).astype(yc_ref.dtype)
        return carry

    lax.fori_loop(0, tc // CHUNK, chunk, 0)

    @pl.when(it == pl.num_programs(2) - 1)
    def _():
        so_ref[...] = st_ref[...].T


def _prompt_hgrn(proj, lb, ng, B, T, tc):
    nt = T // tc
    per = BRANCH_W // LANES
    blk = lambda cb: pl.BlockSpec((tc, LANES), lambda b, h, i, cb=cb: (b * nt + i, cb * per + h))
    vec = pl.BlockSpec((1, LANES), lambda b, h, i: (0, h))
    return pl.pallas_call(
        _hgrn_body,
        grid=(B, C_HEADS, nt),
        in_specs=[blk(CB_CQ), blk(CB_CF), blk(CB_CI), blk(CB_CG), vec, vec],
        out_specs=[pl.BlockSpec((tc, LANES), lambda b, h, i: (b * nt + i, h)),
                   pl.BlockSpec((None, None, C_DK, C_DK), lambda b, h, i: (b, h, 0, 0))],
        out_shape=[jax.ShapeDtypeStruct((B * T, BRANCH_W), BF), jax.ShapeDtypeStruct((B, C_HEADS, C_DK, C_DK), F32)],
        scratch_shapes=[pltpu.VMEM((C_DK, C_DK), F32)],
        compiler_params=_cp(("parallel", "parallel", "arbitrary")),
        name="hgrn_prompt",
    )(proj, proj, proj, proj, lb, ng)


def _merge_body(ya_ref, yb_ref, yc_ref, yd_ref, *rest):
    g_refs = rest[:2 * N_BRANCH]
    x_ref, wb_ref, wo_ref, lg_ref, lbias_ref, h_ref = rest[2 * N_BRANCH:]
    halves = [jnp.zeros((x_ref.shape[0], BRANCH_W), F32) for _ in range(2)]
    for n, y_ref in enumerate((ya_ref, yb_ref, yc_ref, yd_ref)):
        br = jnp.dot(y_ref[...].astype(BF), wb_ref[n], preferred_element_type=F32)
        for hh in range(2):
            gate = _sigmoid(g_refs[2 * n + hh][...])
            halves[hh] = halves[hh] + gate * br[:, hh * BRANCH_W:(hh + 1) * BRANCH_W]
    mixed = jnp.concatenate(halves, axis=1).astype(BF)
    pre = DN_ALPHA * x_ref[...] + jnp.dot(mixed, wo_ref[...], preferred_element_type=F32)
    h_ref[...] = _layer_norm(pre, lg_ref[...], lbias_ref[...])


def _merge(ys, proj, x, wb_bf, wo_bf, ln_g, ln_b, tm):
    M = x.shape[0]
    yspec = pl.BlockSpec((tm, BRANCH_W), lambda i: (i, 0))
    gspec = lambda c: pl.BlockSpec((tm, BRANCH_W), lambda i, c=c: (i, CB_GATE + c))
    xspec = pl.BlockSpec((tm, D_MODEL), lambda i: (i, 0))
    const = lambda shape: pl.BlockSpec(shape, lambda i, n=len(shape): (0,) * n)
    return pl.pallas_call(
        _merge_body,
        grid=(M // tm,),
        in_specs=[yspec] * N_BRANCH + [gspec(c) for c in range(2 * N_BRANCH)]
        + [xspec, const((N_BRANCH, BRANCH_W, D_MODEL)), const((D_MODEL, D_MODEL)), const((1, D_MODEL)), const((1, D_MODEL))],
        out_specs=xspec,
        out_shape=jax.ShapeDtypeStruct((M, D_MODEL), F32),
        compiler_params=_cp(("parallel",)),
        name="merge_ln1",
    )(*ys, *([proj] * (2 * N_BRANCH)), x, wb_bf, wo_bf, ln_g, ln_b)


def _ffn_finish(gate, val, wd_ref, h_ref, lg_ref, lb_ref, o_ref):
    a = (_silu(gate) * val).astype(BF)
    pre = DN_ALPHA * h_ref[...] + jnp.dot(a, wd_ref[...], preferred_element_type=F32)
    o_ref[...] = _layer_norm(pre, lg_ref[...], lb_ref[...])


def _ffn_body(ug_ref, uv_ref, hg_ref, hv_ref, cwg_ref, cwv_ref, wd_ref, h_ref, lg_ref, lb_ref, o_ref, eg_ref, ev_ref):
    tm = ug_ref.shape[0]
    first = pl.program_id(1) == 0

    def conv(u_ref, halo_ref, cw_ref, ext_ref):
        u = u_ref[...]
        ext_ref[0:8, :] = jnp.where(first, 0.0, halo_ref[...])
        ext_ref[8:, :] = u
        cw = cw_ref[...]
        return cw[0:1] * ext_ref[6:6 + tm, :] + cw[1:2] * ext_ref[7:7 + tm, :] + cw[2:3] * u

    gate = conv(ug_ref, hg_ref, cwg_ref, eg_ref)
    val = conv(uv_ref, hv_ref, cwv_ref, ev_ref)
    _ffn_finish(gate, val, wd_ref, h_ref, lg_ref, lb_ref, o_ref)


def _prompt_ffn(up, ffn_conv, wd_bf, h, ln_g, ln_b, B, T, tm):
    nt = T // tm
    row = lambda b, i: b * nt + i
    blk = lambda c: pl.BlockSpec((tm, D_FF), lambda b, i, c=c: (row(b, i), c))
    halo = lambda c: pl.BlockSpec((8, D_FF), lambda b, i, c=c: (jnp.maximum(row(b, i) * (tm // 8) - 1, 0), c))
    cw = lambda c: pl.BlockSpec((FFN_CONV_W, D_FF), lambda b, i, c=c: (0, c))
    const = lambda shape: pl.BlockSpec(shape, lambda b, i, n=len(shape): (0,) * n)
    xspec = pl.BlockSpec((tm, D_MODEL), lambda b, i: (row(b, i), 0))
    return pl.pallas_call(
        _ffn_body,
        grid=(B, nt),
        in_specs=[blk(0), blk(1), halo(0), halo(1), cw(0), cw(1), const((D_FF, D_MODEL)), xspec,
                  const((1, D_MODEL)), const((1, D_MODEL))],
        out_specs=xspec,
        out_shape=jax.ShapeDtypeStruct((B * T, D_MODEL), F32),
        scratch_shapes=[pltpu.VMEM((tm + 8, D_FF), F32), pltpu.VMEM((tm + 8, D_FF), F32)],
        compiler_params=_cp(("parallel", "arbitrary")),
        name="ffn_tail_prompt",
    )(up, up, up, up, ffn_conv, ffn_conv, wd_bf, h, ln_g, ln_b)


def _ffn_sample_body(g0_ref, g1_ref, g2_ref, v0_ref, v1_ref, v2_ref, cwg_ref, cwv_ref, wd_ref, h_ref, lg_ref, lb_ref, o_ref):
    cwg, cwv = cwg_ref[...], cwv_ref[...]
    gate = cwg[0:1] * g0_ref[...] + cwg[1:2] * g1_ref[...] + cwg[2:3] * g2_ref[...]
    val = cwv[0:1] * v0_ref[...] + cwv[1:2] * v1_ref[...] + cwv[2:3] * v2_ref[...]
    _ffn_finish(gate, val, wd_ref, h_ref, lg_ref, lb_ref, o_ref)


def _sample_ffn(full, ffn_conv, wd_bf, h, ln_g, ln_b, Bd, Ts):
    blk = lambda c, r: pl.BlockSpec((Bd, D_FF), lambda t, c=c, r=r: (t + r, c))
    cw = lambda c: pl.BlockSpec((FFN_CONV_W, D_FF), lambda t, c=c: (0, c))
    const = lambda shape: pl.BlockSpec(shape, lambda t, n=len(shape): (0,) * n)
    xspec = pl.BlockSpec((Bd, D_MODEL), lambda t: (t, 0))
    return pl.pallas_call(
        _ffn_sample_body,
        grid=(Ts,),
        in_specs=[blk(0, 0), blk(0, 1), blk(0, 2), blk(1, 0), blk(1, 1), blk(1, 2), cw(0), cw(1),
                  const((D_FF, D_MODEL)), xspec, const((1, D_MODEL)), const((1, D_MODEL))],
        out_specs=xspec,
        out_shape=jax.ShapeDtypeStruct((Ts * Bd, D_MODEL), F32),
        compiler_params=_cp(("parallel",)),
        name="ffn_tail_sample",
    )(full, full, full, full, full, full, ffn_conv, ffn_conv, wd_bf, h, ln_g, ln_b)


def _sattn_body(pt_ref, q_ref, kn_ref, vn_ref, *rest, n_pages, pos0, ts):
    del pt_ref
    kp = rest[:n_pages]
    vp = rest[n_pages:2 * n_pages]
    o_ref = rest[2 * n_pages]
    qr = q_ref[0]
    nrow = qr.shape[0]
    ppb = MOBA_BLOCK // PAGE_SIZE
    n_past = n_pages // ppb
    sums = []
    for n in range(n_past):
        acc = jnp.sum(kp[n * ppb][...], axis=0, keepdims=True)
        for r in range(1, ppb):
            acc = acc + jnp.sum(kp[n * ppb + r][...], axis=0, keepdims=True)
        sums.append(acc * (1.0 / MOBA_BLOCK))
    kmean = jnp.concatenate(sums + [jnp.zeros((LANES - n_past, BRANCH_W), F32)], axis=0)
    gate = lax.dot_general(qr, kmean, NT, precision=lax.Precision.HIGHEST, preferred_element_type=F32)
    bias = _top_blocks_bias(gate, n_past)
    qb = (qr * (A_DH ** -0.5)).astype(BF)
    scores = []
    for p in range(n_pages):
        s = lax.dot_general(qb, kp[p][...].astype(BF), NT, preferred_element_type=F32)
        scores.append(s + bias[:, p // ppb:p // ppb + 1])
    pad = jnp.zeros((LANES - kn_ref.shape[1], BRANCH_W), F32)
    kn = jnp.concatenate([kn_ref[0], pad], axis=0).astype(BF)
    vn = jnp.concatenate([vn_ref[0], pad], axis=0).astype(BF)
    s_own = lax.dot_general(qb, kn, NT, preferred_element_type=F32)
    qi = lax.broadcasted_iota(jnp.int32, (nrow, LANES), 0) // A_HEADS
    kj = lax.broadcasted_iota(jnp.int32, (nrow, LANES), 1)
    own_blk = pos0 // MOBA_BLOCK
    ok = (kj < ts) & (own_blk * MOBA_BLOCK + kj <= pos0 + qi)
    s_own = jnp.where(ok, s_own, NEG_INF)
    m = jnp.max(s_own, axis=1, keepdims=True)
    for s in scores:
        m = jnp.maximum(m, jnp.max(s, axis=1, keepdims=True))
    p_own = jnp.exp(s_own - m)
    l = jnp.sum(p_own, axis=1, keepdims=True)
    out = jnp.dot(p_own.astype(BF), vn, preferred_element_type=F32)
    for p in range(n_pages):
        pr = jnp.exp(scores[p] - m)
        l = l + jnp.sum(pr, axis=1, keepdims=True)
        out = out + jnp.dot(pr.astype(BF), vp[p][...].astype(BF), preferred_element_type=F32)
    out = out / l
    hrow = lax.broadcasted_iota(jnp.int32, (nrow, BRANCH_W), 0) % A_HEADS
    hlane = lax.broadcasted_iota(jnp.int32, (nrow, BRANCH_W), 1) // A_DH
    out = jnp.where(hrow == hlane, out, 0.0)
    o_ref[0] = jnp.sum(out.reshape(ts, A_HEADS, BRANCH_W), axis=1)


def _sample_attention(qrows, kn, vn, cache_k, cache_v, page_table, layer, pos0, ts):
    Bd, n_pages = page_table.shape
    nrow = qrows.shape[1]
    page = lambda p: pl.BlockSpec((None, None, PAGE_SIZE, BRANCH_W),
                                  lambda b, pt, p=p: (layer, pt[b * n_pages + p], 0, 0))
    req = lambda r: pl.BlockSpec((1, r, BRANCH_W), lambda b, pt: (b, 0, 0))
    grid_spec = pltpu.PrefetchScalarGridSpec(
        num_scalar_prefetch=1,
        grid=(Bd,),
        in_specs=[req(nrow), req(kn.shape[1]), req(vn.shape[1])] + [page(p) for p in range(n_pages)] * 2,
        out_specs=req(ts),
    )
    return pl.pallas_call(
        functools.partial(_sattn_body, n_pages=n_pages, pos0=pos0, ts=ts),
        grid_spec=grid_spec,
        out_shape=jax.ShapeDtypeStruct((Bd, ts, BRANCH_W), F32),
        compiler_params=_cp(("parallel",)),
        name="moba_sample",
    )(page_table.reshape(-1), qrows, kn, vn, *([cache_k] * n_pages), *([cache_v] * n_pages))


def _smix_body(uf_ref, db_ref, dc_ref, dh_ref, zb_ref, pw_ref, ps_ref, sw_ref, yb_ref, yd_ref, zn_ref, *, bd, ts, pos0):
    ps, sw = ps_ref[...], sw_ref[...]
    slab = lambda ref, r: ref[r * bd:(r + 1) * bd, :]
    z = [slab(zb_ref, r) for r in range(SCONV_W - 1)] + [slab(dc_ref, t) * slab(dh_ref, t) for t in range(ts)]
    for t in range(ts):
        u = slab(uf_ref, POOL_BUF + t)
        for g, w in enumerate(POOL_WINDOWS):
            sl = slice(g * POOL_GC, (g + 1) * POOL_GC)
            win = u[:, sl]
            for r in range(1, w):
                win = win + slab(uf_ref, POOL_BUF + t - r)[:, sl]
            pooled = win / float(min(w, pos0 + t + 1)) - u[:, sl]
            y = jnp.dot(pooled.astype(BF), pw_ref[g], preferred_element_type=F32) * ps[:, sl]
            yb_ref[t * bd:(t + 1) * bd, sl] = y.astype(yb_ref.dtype)
        conv = sw[0:1] * z[t] + sw[1:2] * z[t + 1] + sw[2:3] * z[t + 2]
        yd_ref[t * bd:(t + 1) * bd, :] = (slab(db_ref, t) * conv).astype(yd_ref.dtype)
    for r in range(SCONV_W - 1):
        zn_ref[r * bd:(r + 1) * bd, :] = z[ts + r]


def _sample_mixers(ufull, proj, zbuf, pool_w_bf, pool_scale, sconv_w, Bd, Ts, pos0):
    M = Ts * Bd
    whole = lambda a: pl.BlockSpec(a.shape, lambda i, n=a.ndim: (0,) * n)
    blk = lambda cb: pl.BlockSpec((M, BRANCH_W), lambda i, cb=cb: (0, cb))
    out = lambda r, dt: (pl.BlockSpec((r, BRANCH_W), lambda i: (0, 0)), jax.ShapeDtypeStruct((r, BRANCH_W), dt))
    outs = [out(M, BF), out(M, BF), out((SCONV_W - 1) * Bd, F32)]
    return pl.pallas_call(
        functools.partial(_smix_body, bd=Bd, ts=Ts, pos0=pos0),
        grid=(1,),
        in_specs=[whole(ufull), blk(CB_DB), blk(CB_DC), blk(CB_DH), whole(zbuf), whole(pool_w_bf), whole(pool_scale),
                  whole(sconv_w)],
        out_specs=[o[0] for o in outs],
        out_shape=[o[1] for o in outs],
        compiler_params=_cp(("arbitrary",)),
        name="pool_sconv_sample",
    )(ufull, proj, proj, proj, zbuf, pool_w_bf, pool_scale, sconv_w)


def _shgrn_body(cq_ref, cf_ref, ci_ref, cg_ref, s0_ref, lb_ref, ng_ref, yc_ref, so_ref, *, ts):
    lb, ng = lb_ref[...], ng_ref[...]
    for h in range(C_HEADS):
        sl = slice(h * C_DK, (h + 1) * C_DK)
        o, st = _gla_chunk(cq_ref[0, :, sl], cf_ref[0, :, sl], ci_ref[0, :, sl], lb[:, sl], s0_ref[0, h].T, ts)
        so_ref[0, h] = st.T
        yc_ref[0, :, sl] = _hgrn_out(o, cg_ref[0, :, sl], ng[:, sl])


def _sample_hgrn(cq, cf, ci, cg, s0, lb, ng, ts):
    Bd, rows, _ = cq.shape
    req = pl.BlockSpec((1, rows, BRANCH_W), lambda b: (b, 0, 0))
    st = pl.BlockSpec((1, C_HEADS, C_DK, C_DK), lambda b: (b, 0, 0, 0))
    vec = pl.BlockSpec((1, BRANCH_W), lambda b: (0, 0))
    return pl.pallas_call(
        functools.partial(_shgrn_body, ts=ts),
        grid=(Bd,),
        in_specs=[req, req, req, req, st, vec, vec],
        out_specs=[req, st],
        out_shape=[jax.ShapeDtypeStruct((Bd, rows, BRANCH_W), F32), jax.ShapeDtypeStruct(s0.shape, F32)],
        compiler_params=_cp(("parallel",)),
        name="hgrn_sample",
    )(cq, cf, ci, cg, s0, lb, ng)


def _prompt_layer(x, B, T, w, lb, tables):
    tm = min(T, 1024)
    proj = _matmul(x, w["w_in"], tm, BRANCH_W, "proj_prompt")
    qf, qb, kf, kb, vb = _rope(proj, tables, min(T, 512))
    km = _kmean(kf, B, T)
    nb = T // MOBA_BLOCK
    npair = BRANCH_W // LANES
    kmp = jnp.pad(km.reshape(B, nb, npair, LANES).transpose(0, 2, 1, 3), ((0, 0), (0, 0), (0, LANES - nb), (0, 0)))
    ya = _prompt_attention(qf, qb, kb, vb, kmp, B, T)
    yb, yd, ztail = _prompt_mixers(proj, w["pool_w"], w["pool_scale"], w["sconv_w"], B, T, min(T, 512))
    yc, s_new = _prompt_hgrn(proj, lb, w["hgrn_norm_g"], B, T, min(T, 512))
    h = _merge((ya, yb, yc, yd), proj, x, w["w_branch"], w["w_o"], w["ln1_g"], w["ln1_b"], 256)
    up = _matmul(h, w["ffn_up"], tm, BRANCH_W, "ffn_up_prompt")
    out = _prompt_ffn(up, w["ffn_conv"], w["ffn_down"], h, w["ln2_g"], w["ln2_b"], B, T, 256)
    proj3 = proj.reshape(B, T, N_IN)
    k = kf.reshape(B, T, A_HEADS, A_DH)
    v = proj3[:, :, CB_AV * BRANCH_W:(CB_AV + 1) * BRANCH_W].reshape(B, T, A_HEADS, A_DH)
    new_pool = proj3[:, T - POOL_BUF:, CB_PU * BRANCH_W:(CB_PU + 1) * BRANCH_W]
    new_sconv = ztail[:, 8 - (SCONV_W - 1):]
    new_ffn = up.reshape(B, T, 2 * D_FF)[:, T - (FFN_CONV_W - 1):]
    return out, k, v, new_pool, s_new, new_sconv, new_ffn


def _to_tm(a):
    return a.transpose(1, 0, 2).reshape(a.shape[1] * a.shape[0], a.shape[2])


def _from_tm(a, Bd):
    return a.reshape(a.shape[0] // Bd, Bd, a.shape[1]).transpose(1, 0, 2)


def _sample_layer(x, Bd, Ts, pos0, layer, w, lb, tables, cache_k, cache_v, page_table, pool_buf, hgrn_s, sconv_buf, ffn_buf):
    M = Ts * Bd
    proj = _matmul(x, w["w_in"], M, BRANCH_W, "proj_sample")
    qf, _, kf, _, _ = _rope(proj, tables, M)
    col = lambda cb: proj[:, cb * BRANCH_W:(cb + 1) * BRANCH_W]
    q_r, k_r, v_r = _from_tm(qf, Bd), _from_tm(kf, Bd), _from_tm(col(CB_AV), Bd)
    head_of_lane = jnp.arange(BRANCH_W) // A_DH
    hmask = (head_of_lane[None, :] == jnp.arange(A_HEADS)[:, None]).astype(F32)
    qrows = (q_r[:, :, None, :] * hmask[None, None]).reshape(Bd, Ts * A_HEADS, BRANCH_W)
    pad8 = lambda a: jnp.pad(a, ((0, 0), (0, 8 - Ts), (0, 0)))
    ya = _sample_attention(qrows, pad8(k_r), pad8(v_r), cache_k, cache_v, page_table, layer, pos0, Ts)
    ufull = jnp.concatenate([_to_tm(pool_buf), col(CB_PU)], axis=0)
    yb, yd, znew = _sample_mixers(ufull, proj, _to_tm(sconv_buf), w["pool_w"], w["pool_scale"], w["sconv_w"], Bd, Ts, pos0)
    req8 = lambda cb: pad8(_from_tm(col(cb), Bd))
    yc, s_new = _sample_hgrn(req8(CB_CQ), req8(CB_CF), req8(CB_CI), req8(CB_CG), hgrn_s, lb, w["hgrn_norm_g"], Ts)
    ys = (_to_tm(ya), yb, _to_tm(yc[:, :Ts]), yd)
    h = _merge(ys, proj, x, w["w_branch"], w["w_o"], w["ln1_g"], w["ln1_b"], min(M, 256))
    up = _matmul(h, w["ffn_up"], M, BRANCH_W, "ffn_up_sample")
    full = jnp.concatenate([_to_tm(ffn_buf), up], axis=0)
    out = _sample_ffn(full, w["ffn_conv"], w["ffn_down"], h, w["ln2_g"], w["ln2_b"], Bd, Ts)
    k = k_r.reshape(Bd, Ts, A_HEADS, A_DH)
    v = v_r.reshape(Bd, Ts, A_HEADS, A_DH)
    new_pool = _from_tm(ufull[Ts * Bd:], Bd)
    new_sconv = _from_tm(znew, Bd)
    new_ffn = _from_tm(full[Ts * Bd:], Bd)
    return out, k, v, new_pool, s_new, new_sconv, new_ffn


def _layer_weights(l, w_in, w_branch, w_o, pool_w, pool_scale, hgrn_norm_g, sconv_w, ln1_g, ln1_b, ffn_up, ffn_conv,
                   ffn_down, ln2_g, ln2_b):
    row = lambda a: a[l][None, :]
    return {
        "w_in": w_in[l].astype(BF), "w_branch": w_branch[l].astype(BF), "w_o": w_o[l].astype(BF),
        "pool_w": pool_w[l].astype(BF), "pool_scale": row(pool_scale), "hgrn_norm_g": row(hgrn_norm_g),
        "sconv_w": sconv_w[l], "ln1_g": row(ln1_g), "ln1_b": row(ln1_b), "ffn_up": ffn_up[l].astype(BF),
        "ffn_conv": ffn_conv[l], "ffn_down": ffn_down[l].astype(BF), "ln2_g": row(ln2_g), "ln2_b": row(ln2_b),
    }


def kernel(x_prompt, x_sample, cache_k, cache_v, state_pool, state_hgrn, state_sconv, state_ffn, page_table, w_in,
           w_branch, w_o, pool_w, pool_scale, hgrn_lb_logits, hgrn_norm_g, sconv_w, ln1_g, ln1_b, ffn_up, ffn_conv,
           ffn_down, ln2_g, ln2_b):
    depth = w_in.shape[0]
    Bp, T, _ = x_prompt.shape
    Bd, Ts, _ = x_sample.shape
    n_pages = page_table.shape[1]
    pos0 = n_pages * PAGE_SIZE
    assert T % MOBA_BLOCK == 0 and pos0 % MOBA_BLOCK == 0 and Ts <= 8 and T >= HALO
    lb_all = _lower_bounds(hgrn_lb_logits.astype(F32))
    tab_p = _rope_tables(jnp.arange(T))
    tab_s = _rope_tables(pos0 + jnp.repeat(jnp.arange(Ts), Bd))
    ck = cache_k.reshape(depth, cache_k.shape[1], PAGE_SIZE, BRANCH_W)
    cv = cache_v.reshape(depth, cache_v.shape[1], PAGE_SIZE, BRANCH_W)
    hp = x_prompt.reshape(Bp * T, D_MODEL)
    hs = _to_tm(x_sample)
    outs_p, outs_s = [], []
    for l in range(depth):
        w = _layer_weights(l, w_in, w_branch, w_o, pool_w, pool_scale, hgrn_norm_g, sconv_w, ln1_g, ln1_b, ffn_up,
                           ffn_conv, ffn_down, ln2_g, ln2_b)
        lb = lb_all[l][None, :]
        rp = _prompt_layer(hp, Bp, T, w, lb, tab_p)
        rs = _sample_layer(hs, Bd, Ts, pos0, l, w, lb, tab_s, ck, cv, page_table, state_pool[l], state_hgrn[l],
                           state_sconv[l], state_ffn[l])
        hp, hs = rp[0], rs[0]
        outs_p.append(rp[1:])
        outs_s.append(rs[1:])
    stack = lambda outs: tuple(jnp.stack([o[j] for o in outs]) for j in range(6))
    return (hp.reshape(Bp, T, D_MODEL), _from_tm(hs, Bd)) + stack(outs_p) + stack(outs_s)
```

```python
import functools
import math

import jax
import jax.numpy as jnp
import numpy as np
from jax import lax
from jax.experimental import pallas as pl
from jax.experimental.pallas import tpu as pltpu

F32 = jnp.float32
BF = jnp.bfloat16
NEG_INF = float("-inf")
LOG2_E = math.log2(math.e)

D_MODEL = 1024
BRANCH_W = 512
N_BRANCH = 4
A_HEADS = 8
A_DH = 64
MOBA_BLOCK = 256
MOBA_TOPK = 3
ROPE_DIM = A_DH // 4
ROPE_THETA = 500000.0
POOL_WINDOWS = (2, 4, 8, 16)
POOL_GC = 128
POOL_BUF = 15
C_HEADS = 4
C_DK = 128
SCONV_W = 3
D_FF = 2816
FFN_CONV_W = 3
PAGE_SIZE = 128
DEPTH = 4
DN_ALPHA = (2 * DEPTH) ** 0.25
LN_EPS = 1e-5
RMS_EPS = 1e-6

CB_AQ, CB_AK, CB_AV, CB_PU, CB_CQ, CB_CF, CB_CI, CB_CG, CB_DB, CB_DC, CB_DH, CB_GATE = range(12)
N_IN = (CB_GATE + 2 * N_BRANCH) * BRANCH_W

LANES = 128
HALO = 16
CHUNK = 16
VMEM_LIMIT = 56 * 1024 * 1024

NT = (((1,), (1,)), ((), ()))
TN = (((0,), (0,)), ((), ()))


def _cp(sem, vmem=VMEM_LIMIT):
    return pltpu.CompilerParams(dimension_semantics=sem, vmem_limit_bytes=vmem)


def _sigmoid(x):
    return 1.0 / (1.0 + jnp.exp(-x))


def _silu(x):
    return x * _sigmoid(x)


def _layer_norm(x, g, b):
    mu = jnp.mean(x, axis=-1, keepdims=True)
    xc = x - mu
    var = jnp.mean(xc * xc, axis=-1, keepdims=True)
    return xc * lax.rsqrt(var + LN_EPS) * g + b


def _lb_body(logit_ref, o_ref):
    x = logit_ref[...]
    m = jnp.max(x, axis=0, keepdims=True)
    e = jnp.exp(x - m)
    p = e / jnp.sum(e, axis=0, keepdims=True)
    acc = jnp.zeros_like(p[0:1])
    rows = []
    for l in range(x.shape[0]):
        acc = acc + p[l:l + 1]
        rows.append(acc - p[0:1])
    o_ref[...] = jnp.concatenate(rows, axis=0)


def _lower_bounds(logits):
    return pl.pallas_call(_lb_body, out_shape=jax.ShapeDtypeStruct(logits.shape, F32), name="hgrn_lb")(logits)


def _mm_body(x_ref, w_ref, o_ref, xb_ref):
    @pl.when(pl.program_id(1) == 0)
    def _():
        xb_ref[...] = x_ref[...].astype(BF)

    o_ref[...] = jnp.dot(xb_ref[...], w_ref[...], preferred_element_type=F32)


def _matmul(x, w_bf, tm, tn, name):
    M, K = x.shape
    N = w_bf.shape[1]
    assert M % tm == 0 and N % tn == 0
    return pl.pallas_call(
        _mm_body,
        grid=(M // tm, N // tn),
        in_specs=[pl.BlockSpec((tm, K), lambda i, j: (i, 0)),
                  pl.BlockSpec((K, tn), lambda i, j: (0, j))],
        out_specs=pl.BlockSpec((tm, tn), lambda i, j: (i, j)),
        out_shape=jax.ShapeDtypeStruct((M, N), F32),
        scratch_shapes=[pltpu.VMEM((tm, K), BF)],
        compiler_params=_cp(("parallel", "arbitrary")),
        name=name,
    )(x, w_bf)


def _rope_tables(pos):
    half = ROPE_DIM // 2
    inv = jnp.power(ROPE_THETA, -jnp.arange(0, ROPE_DIM, 2, dtype=F32) / ROPE_DIM)
    ang = pos.astype(F32)[:, None] * inv[None, :]
    cos, sin = jnp.cos(ang), jnp.sin(ang)
    n = pos.shape[0]
    ones = jnp.ones((n, A_DH - ROPE_DIM), F32)
    zeros = jnp.zeros((n, A_DH - ROPE_DIM), F32)
    zh = jnp.zeros((n, half), F32)
    c = jnp.concatenate([cos, cos, ones], axis=1)
    s_lo = jnp.concatenate([-sin, zh, zeros], axis=1)
    s_hi = jnp.concatenate([zh, sin, zeros], axis=1)
    tile = lambda a: jnp.tile(a, (1, A_HEADS))
    return tile(c), tile(s_lo), tile(s_hi)


def _rope_body(q_ref, k_ref, v_ref, c_ref, sl_ref, sh_ref, qf_ref, qb_ref, kf_ref, kb_ref, vb_ref):
    c, sl, sh = c_ref[...], sl_ref[...], sh_ref[...]
    half = ROPE_DIM // 2
    w = q_ref.shape[1]

    def rope(x):
        return x * c + pltpu.roll(x, w - half, 1) * sl + pltpu.roll(x, half, 1) * sh

    q = rope(q_ref[...])
    k = rope(k_ref[...])
    qf_ref[...] = q
    qb_ref[...] = (q * (A_DH ** -0.5)).astype(BF)
    kf_ref[...] = k
    kb_ref[...] = k.astype(BF)
    vb_ref[...] = v_ref[...].astype(BF)


def _rope(proj, tables, tm):
    M = proj.shape[0]
    ntab = tables[0].shape[0] // tm
    blk = lambda cb: pl.BlockSpec((tm, BRANCH_W), lambda i, cb=cb: (i, cb))
    tab = pl.BlockSpec((tm, BRANCH_W), lambda i: (i % ntab, 0))
    out = pl.BlockSpec((tm, BRANCH_W), lambda i: (i, 0))
    sds = lambda dt: jax.ShapeDtypeStruct((M, BRANCH_W), dt)
    return pl.pallas_call(
        _rope_body,
        grid=(M // tm,),
        in_specs=[blk(CB_AQ), blk(CB_AK), blk(CB_AV), tab, tab, tab],
        out_specs=[out] * 5,
        out_shape=[sds(F32), sds(BF), sds(F32), sds(BF), sds(BF)],
        compiler_params=_cp(("parallel",)),
        name="rope",
    )(proj, proj, proj, *tables)


def _top_blocks_bias(gate, n_past):
    col = lax.broadcasted_iota(jnp.int32, gate.shape, 1)
    g = jnp.where(col < n_past, gate, NEG_INF)
    bias = jnp.full(gate.shape, NEG_INF, F32)
    for _ in range(MOBA_TOPK):
        m = jnp.max(g, axis=1, keepdims=True)
        idx = jnp.min(jnp.where(g == m, col, LANES), axis=1, keepdims=True)
        pick = (col == idx) & (m > NEG_INF)
        bias = jnp.where(pick, 0.0, bias)
        g = jnp.where(pick, NEG_INF, g)
    return bias


def _split_bf16(x):
    hi = x.astype(BF)
    return hi, (x - hi.astype(F32)).astype(BF)


def _prep_body(q_ref, k_ref, v_ref, c_ref, sl_ref, sh_ref, kf_ref, kb_ref, qt_ref, vt_ref, bias_ref, km_ref):
    i = pl.program_id(1)
    nbp = km_ref.shape[0]

    @pl.when(i == 0)
    def _():
        km_ref[...] = jnp.zeros_like(km_ref)

    c, sl, sh = c_ref[...], sl_ref[...], sh_ref[...]
    half = ROPE_DIM // 2

    def rope(x):
        return x * c + pltpu.roll(x, BRANCH_W - half, 1) * sl + pltpu.roll(x, half, 1) * sh

    q = rope(q_ref[...])
    k = rope(k_ref[...])
    kf_ref[...] = k
    kb_ref[...] = k.astype(BF)
    qt_ref[...] = (q * (A_DH ** -0.5 * LOG2_E)).T.astype(BF)
    vt_ref[...] = v_ref[...].T.astype(BF)

    km = km_ref[...]
    lane = lax.broadcasted_iota(jnp.int32, km.shape, 1)
    stack = jnp.concatenate([jnp.where(lane // A_DH == h, km, 0.0) for h in range(A_HEADS)], axis=0)
    k_hi, k_lo = _split_bf16(stack)
    q_hi, q_lo = _split_bf16(q)
    dot = lambda a, b: lax.dot_general(a, b, NT, preferred_element_type=F32)
    gate = (dot(k_hi, q_hi) + dot(k_lo, q_hi) + dot(k_hi, q_lo)).reshape(A_HEADS, nbp, MOBA_BLOCK)
    blk = lax.broadcasted_iota(jnp.int32, gate.shape, 1)
    gate = jnp.where(blk < i, gate, NEG_INF)
    bias = jnp.full(gate.shape, NEG_INF, F32)
    for _ in range(MOBA_TOPK):
        m = jnp.max(gate, axis=1, keepdims=True)
        idx = jnp.min(jnp.where(gate == m, blk, nbp), axis=1, keepdims=True)
        pick = (blk == idx) & (m > NEG_INF)
        bias = jnp.where(pick, 0.0, bias)
        gate = jnp.where(pick, NEG_INF, gate)
    bias_ref[0] = bias.reshape(A_HEADS * nbp, MOBA_BLOCK)
    km_ref[pl.ds(i, 1), :] = jnp.sum(k, axis=0, keepdims=True) * (1.0 / MOBA_BLOCK)


def _attn_prep(proj, tables, B, T):
    nq = T // MOBA_BLOCK
    nbp = -(-nq // 8) * 8
    tq = MOBA_BLOCK
    blk = lambda cb: pl.BlockSpec((tq, BRANCH_W), lambda b, i, cb=cb: (b * nq + i, cb))
    tab = pl.BlockSpec((tq, BRANCH_W), lambda b, i: (i, 0))
    row = pl.BlockSpec((tq, BRANCH_W), lambda b, i: (b * nq + i, 0))
    tr = pl.BlockSpec((BRANCH_W, tq), lambda b, i: (b, i))
    return pl.pallas_call(
        _prep_body,
        grid=(B, nq),
        in_specs=[blk(CB_AQ), blk(CB_AK), blk(CB_AV), tab, tab, tab],
        out_specs=[row, row, tr, tr, pl.BlockSpec((1, A_HEADS * nbp, tq), lambda b, i: (b, 0, i))],
        out_shape=[jax.ShapeDtypeStruct((B * T, BRANCH_W), F32), jax.ShapeDtypeStruct((B * T, BRANCH_W), BF),
                   jax.ShapeDtypeStruct((B * BRANCH_W, T), BF), jax.ShapeDtypeStruct((B * BRANCH_W, T), BF),
                   jax.ShapeDtypeStruct((B, A_HEADS * nbp, T), F32)],
        scratch_shapes=[pltpu.VMEM((nbp, BRANCH_W), F32)],
        compiler_params=_cp(("parallel", "arbitrary")),
        name="attn_prep",
    )(proj, proj, proj, *tables)


def _attn_body(qt_ref, k_ref, vt_ref, bias_ref, o_ref):
    i = pl.program_id(2)
    tq = MOBA_BLOCK
    npair = LANES // A_DH
    nbp = bias_ref.shape[1] // npair
    qt = qt_ref[...]
    rowi = lax.broadcasted_iota(jnp.int32, qt.shape, 0)
    qts = [jnp.where((rowi >= e * A_DH) & (rowi < (e + 1) * A_DH), qt, jnp.zeros_like(qt)) for e in range(npair)]
    block = lambda j: pl.ds(j * tq if isinstance(j, int) else pl.multiple_of(j * tq, tq), tq)

    def scores(j):
        kj = k_ref[block(j), :]
        return tuple(jnp.dot(kj, qts[e], preferred_element_type=F32) for e in range(npair))

    def consume(j, carry, sc, own):
        new = []
        for e in range(npair):
            m, l, acc = carry[e]
            s = sc[e]
            if own:
                causal = (lax.broadcasted_iota(jnp.int32, (tq, tq), 0) <= lax.broadcasted_iota(jnp.int32, (tq, tq), 1))
                s = jnp.where(causal, s, NEG_INF)
                mn = jnp.maximum(m, jnp.max(s, axis=0, keepdims=True))
                shift = mn
            else:
                b = bias_ref[0, pl.ds(e * nbp + j, 1), :]
                mn = jnp.maximum(m, jnp.max(s, axis=0, keepdims=True) + b)
                shift = mn - b
            a = jnp.exp2(m - mn)
            p = jnp.exp2(s - shift)
            l = a * l + jnp.sum(p, axis=0, keepdims=True)
            acc = a * acc + jnp.dot(vt_ref[e * A_DH:(e + 1) * A_DH, block(j)], p.astype(BF), preferred_element_type=F32)
            new.append((mn, l, acc))
        return tuple(new)

    def body(j, carry):
        state, sc = carry
        nxt = scores(j + 1)
        return consume(j, state, sc, own=False), nxt

    init = tuple((jnp.full((1, tq), -1e30, F32), jnp.zeros((1, tq), F32), jnp.zeros((A_DH, tq), F32))
                 for _ in range(npair))
    state, sc = lax.fori_loop(0, i, body, (init, scores(0)))
    fin = consume(i, state, sc, own=True)
    out_t = jnp.concatenate([acc / l for (_, l, acc) in fin], axis=0)
    o_ref[...] = out_t.T.astype(o_ref.dtype)


def _prompt_attention(qt, kb, vt, bias, B, T):
    nq = T // MOBA_BLOCK
    npair = LANES // A_DH
    ngrp = BRANCH_W // LANES
    nbp = bias.shape[1] // A_HEADS
    return pl.pallas_call(
        _attn_body,
        grid=(B, ngrp, nq),
        in_specs=[pl.BlockSpec((LANES, MOBA_BLOCK), lambda b, hp, i: (b * ngrp + hp, i)),
                  pl.BlockSpec((T, LANES), lambda b, hp, i: (b, hp)),
                  pl.BlockSpec((LANES, T), lambda b, hp, i: (b * ngrp + hp, 0)),
                  pl.BlockSpec((1, npair * nbp, MOBA_BLOCK), lambda b, hp, i: (b, hp, i))],
        out_specs=pl.BlockSpec((MOBA_BLOCK, LANES), lambda b, hp, i: (b * nq + i, hp)),
        out_shape=jax.ShapeDtypeStruct((B * T, BRANCH_W), BF),
        compiler_params=_cp(("parallel", "parallel", "arbitrary")),
        name="moba_prompt",
    )(qt, kb, vt, bias)


def _pool_group(ext_ref, u, g, w, t_first, pw, ps):
    tm = u.shape[0]
    sl = slice(g * POOL_GC, (g + 1) * POOL_GC)
    win = ext_ref[HALO:HALO + tm, sl]
    for r in range(1, w):
        win = win + ext_ref[HALO - r:HALO - r + tm, sl]
    t = t_first + lax.broadcasted_iota(jnp.int32, (tm, POOL_GC), 0)
    cnt = jnp.minimum(w, t + 1).astype(F32)
    pooled = win / cnt - u[:, sl]
    return jnp.dot(pooled.astype(BF), pw, preferred_element_type=F32) * ps[:, sl]


def _mix_body(u_ref, uh_ref, db_ref, dc_ref, dh_ref, dch_ref, dhh_ref, pw_ref, ps_ref, sw_ref,
              yb_ref, yd_ref, zt_ref, ext_ref, zext_ref):
    i = pl.program_id(1)
    tm = u_ref.shape[0]
    first = i == 0
    u = u_ref[...]
    ext_ref[0:HALO, :] = jnp.where(first, 0.0, uh_ref[...])
    ext_ref[HALO:, :] = u
    ps = ps_ref[...]
    for g, w in enumerate(POOL_WINDOWS):
        y = _pool_group(ext_ref, u, g, w, i * tm, pw_ref[g], ps)
        yb_ref[:, g * POOL_GC:(g + 1) * POOL_GC] = y.astype(yb_ref.dtype)
    z = dc_ref[...] * dh_ref[...]
    zext_ref[0:8, :] = jnp.where(first, 0.0, dch_ref[...] * dhh_ref[...])
    zext_ref[8:, :] = z
    sw = sw_ref[...]
    conv = sw[0:1] * zext_ref[6:6 + tm, :] + sw[1:2] * zext_ref[7:7 + tm, :] + sw[2:3] * z
    yd_ref[...] = (db_ref[...] * conv).astype(yd_ref.dtype)
    zt_ref[0] = z[tm - 8:tm]


def _prompt_mixers(proj, pool_w_bf, pool_scale, sconv_w, B, T, tm):
    nt = T // tm
    row = lambda b, i: b * nt + i
    blk = lambda cb: pl.BlockSpec((tm, BRANCH_W), lambda b, i, cb=cb: (row(b, i), cb))

    def halo(cb, h):
        return pl.BlockSpec((h, BRANCH_W), lambda b, i, cb=cb, h=h: (jnp.maximum(row(b, i) * (tm // h) - 1, 0), cb))

    const = lambda shape: pl.BlockSpec(shape, lambda b, i, n=len(shape): (0,) * n)
    return pl.pallas_call(
        _mix_body,
        grid=(B, nt),
        in_specs=[blk(CB_PU), halo(CB_PU, HALO), blk(CB_DB), blk(CB_DC), blk(CB_DH), halo(CB_DC, 8), halo(CB_DH, 8),
                  const((len(POOL_WINDOWS), POOL_GC, POOL_GC)), const((1, BRANCH_W)), const((SCONV_W, BRANCH_W))],
        out_specs=[pl.BlockSpec((tm, BRANCH_W), lambda b, i: (row(b, i), 0)),
                   pl.BlockSpec((tm, BRANCH_W), lambda b, i: (row(b, i), 0)),
                   pl.BlockSpec((1, 8, BRANCH_W), lambda b, i: (b, 0, 0))],
        out_shape=[jax.ShapeDtypeStruct((B * T, BRANCH_W), BF), jax.ShapeDtypeStruct((B * T, BRANCH_W), BF),
                   jax.ShapeDtypeStruct((B, 8, BRANCH_W), F32)],
        scratch_shapes=[pltpu.VMEM((tm + HALO, BRANCH_W), F32), pltpu.VMEM((tm + 8, BRANCH_W), F32)],
        compiler_params=_cp(("parallel", "arbitrary")),
        name="pool_sconv_prompt",
    )(proj, proj, proj, proj, proj, proj, proj, pool_w_bf, pool_scale, sconv_w)


def _gla_chunk(cq, cf, ci, lb, st, n_valid):
    rows = cq.shape[0]
    f = lb + (1.0 - lb) * _sigmoid(cf)
    g = jnp.log(f)
    r_i = lax.broadcasted_iota(jnp.int32, (rows, rows), 0)
    c_i = lax.broadcasted_iota(jnp.int32, (rows, rows), 1)
    tri = jnp.where((r_i >= c_i) & (c_i < n_valid), 1.0, 0.0).astype(F32)
    b = jnp.dot(tri, g, precision=lax.Precision.HIGHEST, preferred_element_type=F32)
    qs = _silu(cq)
    kk = 1.0 - f
    t_i = lax.broadcasted_iota(jnp.int32, (rows, 1), 0)
    o = jnp.zeros((rows, C_DK), F32)
    for s in range(n_valid):
        e = jnp.exp(jnp.minimum(b - b[s:s + 1], 0.0))
        a = jnp.sum(qs * kk[s:s + 1] * e, axis=1, keepdims=True)
        o = o + jnp.where(t_i >= s, a, 0.0) * ci[s:s + 1]
    o = o + lax.dot_general((qs * jnp.exp(b)).astype(BF), st.astype(BF), NT, preferred_element_type=F32)
    bl = b[n_valid - 1:n_valid]
    ks = jnp.where(t_i < n_valid, kk * jnp.exp(jnp.minimum(bl - b, 0.0)), 0.0)
    upd = lax.dot_general(ci.astype(BF), ks.astype(BF), TN, preferred_element_type=F32)
    return o, st * jnp.exp(bl) + upd


def _hgrn_out(o, cg, ng):
    o = o * lax.rsqrt(jnp.mean(o * o, axis=1, keepdims=True) + RMS_EPS)
    return o * ng * _silu(cg)


GROUP = 128
GROUP_HALVES = tuple(GROUP >> (s + 1) for s in range(int(math.log2(GROUP))))


def _decay_sum_matrix():
    t = np.arange(GROUP)[:, None]
    u = np.arange(GROUP)[None, :]
    blocks = [u <= t, u > t]
    for h in GROUP_HALVES:
        r = (t // (2 * h)) * 2 * h + h - 1
        second = (t // h) % 2 == 1
        blocks.append(np.where(second, (u > r) & (u <= t), (u > t) & (u <= r)))
    d = np.concatenate(blocks, axis=0).astype(np.float32)
    return jnp.asarray(np.concatenate([d, d], axis=1), dtype=BF)


def _gla_decays(cf, lb, dmat):
    f = lb + (1.0 - lb) * _sigmoid(cf)
    g_hi, g_lo = _split_bf16(jnp.log(f))
    return f, jnp.dot(dmat, jnp.concatenate([g_hi, g_lo], axis=0), preferred_element_type=F32)


def _gla_group(cq, f, x, ci, st):
    e = jnp.exp(x)
    blk = lambda n: e[n * GROUP:(n + 1) * GROUP]
    qs = _silu(cq)
    kk = 1.0 - f
    v = ci.astype(BF)
    t_i = lax.broadcasted_iota(jnp.int32, (GROUP, GROUP), 0)
    s_i = lax.broadcasted_iota(jnp.int32, (GROUP, GROUP), 1)
    row = lax.broadcasted_iota(jnp.int32, (GROUP, C_DK), 0)
    dot_nt = lambda a, b: lax.dot_general(a.astype(BF), b.astype(BF), NT, preferred_element_type=F32)
    attn = jnp.where(t_i == s_i, dot_nt(qs, kk), 0.0)
    for s, h in enumerate(GROUP_HALVES):
        second = (row // h) % 2 == 1
        scaled_q = jnp.where(second, qs * blk(2 + s), 0.0)
        scaled_k = jnp.where(second, 0.0, kk * blk(2 + s))
        part = dot_nt(scaled_q, scaled_k)
        attn = attn + (part if 2 * h == GROUP else jnp.where(t_i // (2 * h) == s_i // (2 * h), part, 0.0))
    o = jnp.dot(attn.astype(BF), v, preferred_element_type=F32) + dot_nt(qs * blk(0), st)
    upd = lax.dot_general(v, (kk * blk(1)).astype(BF), TN, preferred_element_type=F32)
    return o, st * e[GROUP - 1:GROUP] + upd


def _hgrn_body(cq_ref, cf_ref, ci_ref, cg_ref, lb_ref, ng_ref, dm_ref, yc_ref, so_ref, st_ref):
    it = pl.program_id(1)
    tc = cq_ref.shape[0]

    @pl.when(it == 0)
    def _():
        st_ref[...] = jnp.zeros_like(st_ref)

    lb, ng = lb_ref[...], ng_ref[...]

    def group(gi, carry):
        rows = pl.ds(pl.multiple_of(gi * GROUP, GROUP), GROUP)
        head = lambda h: slice(h * C_DK, (h + 1) * C_DK)
        f, x = _gla_decays(cf_ref[rows, :], lb, dm_ref[...])
        for h in range(C_HEADS):
            sl = head(h)
            o, st = _gla_group(cq_ref[rows, sl], f[:, sl], x[:, sl], ci_ref[rows, sl], st_ref[h])
            st_ref[h] = st
            yc_ref[rows, sl] = _hgrn_out(o, cg_ref[rows, sl], ng[:, sl]).astype(yc_ref.dtype)
        return carry

    lax.fori_loop(0, tc // GROUP, group, 0)

    @pl.when(it == pl.num_programs(1) - 1)
    def _():
        for h in range(C_HEADS):
            so_ref[0, h] = st_ref[h].T


def _prompt_hgrn(proj, lb, ng, B, T, tc):
    nt = T // tc
    dmat = _decay_sum_matrix()
    blk = lambda cb: pl.BlockSpec((tc, BRANCH_W), lambda b, i, cb=cb: (b * nt + i, cb))
    vec = pl.BlockSpec((1, BRANCH_W), lambda b, i: (0, 0))
    return pl.pallas_call(
        _hgrn_body,
        grid=(B, nt),
        in_specs=[blk(CB_CQ), blk(CB_CF), blk(CB_CI), blk(CB_CG), vec, vec,
                  pl.BlockSpec(dmat.shape, lambda b, i: (0, 0))],
        out_specs=[pl.BlockSpec((tc, BRANCH_W), lambda b, i: (b * nt + i, 0)),
                   pl.BlockSpec((1, C_HEADS, C_DK, C_DK), lambda b, i: (b, 0, 0, 0))],
        out_shape=[jax.ShapeDtypeStruct((B * T, BRANCH_W), BF), jax.ShapeDtypeStruct((B, C_HEADS, C_DK, C_DK), F32)],
        scratch_shapes=[pltpu.VMEM((C_HEADS, C_DK, C_DK), F32)],
        compiler_params=_cp(("parallel", "arbitrary")),
        name="hgrn_prompt",
    )(proj, proj, proj, proj, lb, ng, dmat)


def _merge_body(ya_ref, yb_ref, yc_ref, yd_ref, *rest):
    g_refs = rest[:2 * N_BRANCH]
    x_ref, wb_ref, wo_ref, lg_ref, lbias_ref, h_ref = rest[2 * N_BRANCH:]
    halves = [jnp.zeros((x_ref.shape[0], BRANCH_W), F32) for _ in range(2)]
    for n, y_ref in enumerate((ya_ref, yb_ref, yc_ref, yd_ref)):
        br = jnp.dot(y_ref[...].astype(BF), wb_ref[n], preferred_element_type=F32)
        for hh in range(2):
            gate = _sigmoid(g_refs[2 * n + hh][...])
            halves[hh] = halves[hh] + gate * br[:, hh * BRANCH_W:(hh + 1) * BRANCH_W]
    mixed = jnp.concatenate(halves, axis=1).astype(BF)
    pre = DN_ALPHA * x_ref[...] + jnp.dot(mixed, wo_ref[...], preferred_element_type=F32)
    h_ref[...] = _layer_norm(pre, lg_ref[...], lbias_ref[...])


def _merge(ys, proj, x, wb_bf, wo_bf, ln_g, ln_b, tm):
    M = x.shape[0]
    yspec = pl.BlockSpec((tm, BRANCH_W), lambda i: (i, 0))
    gspec = lambda c: pl.BlockSpec((tm, BRANCH_W), lambda i, c=c: (i, CB_GATE + c))
    xspec = pl.BlockSpec((tm, D_MODEL), lambda i: (i, 0))
    const = lambda shape: pl.BlockSpec(shape, lambda i, n=len(shape): (0,) * n)
    return pl.pallas_call(
        _merge_body,
        grid=(M // tm,),
        in_specs=[yspec] * N_BRANCH + [gspec(c) for c in range(2 * N_BRANCH)]
        + [xspec, const((N_BRANCH, BRANCH_W, D_MODEL)), const((D_MODEL, D_MODEL)), const((1, D_MODEL)), const((1, D_MODEL))],
        out_specs=xspec,
        out_shape=jax.ShapeDtypeStruct((M, D_MODEL), F32),
        compiler_params=_cp(("parallel",)),
        name="merge_ln1",
    )(*ys, *([proj] * (2 * N_BRANCH)), x, wb_bf, wo_bf, ln_g, ln_b)


def _ffn_finish(gate, val, wd_ref, h_ref, lg_ref, lb_ref, o_ref):
    a = (_silu(gate) * val).astype(BF)
    pre = DN_ALPHA * h_ref[...] + jnp.dot(a, wd_ref[...], preferred_element_type=F32)
    o_ref[...] = _layer_norm(pre, lg_ref[...], lb_ref[...])


def _ffn_body(ug_ref, uv_ref, hg_ref, hv_ref, cwg_ref, cwv_ref, wd_ref, h_ref, lg_ref, lb_ref, o_ref, eg_ref, ev_ref):
    tm = ug_ref.shape[0]
    first = pl.program_id(1) == 0

    def conv(u_ref, halo_ref, cw_ref, ext_ref):
        u = u_ref[...]
        ext_ref[0:8, :] = jnp.where(first, 0.0, halo_ref[...])
        ext_ref[8:, :] = u
        cw = cw_ref[...]
        return cw[0:1] * ext_ref[6:6 + tm, :] + cw[1:2] * ext_ref[7:7 + tm, :] + cw[2:3] * u

    gate = conv(ug_ref, hg_ref, cwg_ref, eg_ref)
    val = conv(uv_ref, hv_ref, cwv_ref, ev_ref)
    _ffn_finish(gate, val, wd_ref, h_ref, lg_ref, lb_ref, o_ref)


def _prompt_ffn(up, ffn_conv, wd_bf, h, ln_g, ln_b, B, T, tm):
    nt = T // tm
    row = lambda b, i: b * nt + i
    blk = lambda c: pl.BlockSpec((tm, D_FF), lambda b, i, c=c: (row(b, i), c))
    halo = lambda c: pl.BlockSpec((8, D_FF), lambda b, i, c=c: (jnp.maximum(row(b, i) * (tm // 8) - 1, 0), c))
    cw = lambda c: pl.BlockSpec((FFN_CONV_W, D_FF), lambda b, i, c=c: (0, c))
    const = lambda shape: pl.BlockSpec(shape, lambda b, i, n=len(shape): (0,) * n)
    xspec = pl.BlockSpec((tm, D_MODEL), lambda b, i: (row(b, i), 0))
    return pl.pallas_call(
        _ffn_body,
        grid=(B, nt),
        in_specs=[blk(0), blk(1), halo(0), halo(1), cw(0), cw(1), const((D_FF, D_MODEL)), xspec,
                  const((1, D_MODEL)), const((1, D_MODEL))],
        out_specs=xspec,
        out_shape=jax.ShapeDtypeStruct((B * T, D_MODEL), F32),
        scratch_shapes=[pltpu.VMEM((tm + 8, D_FF), F32), pltpu.VMEM((tm + 8, D_FF), F32)],
        compiler_params=_cp(("parallel", "arbitrary")),
        name="ffn_tail_prompt",
    )(up, up, up, up, ffn_conv, ffn_conv, wd_bf, h, ln_g, ln_b)


def _ffn_sample_body(g0_ref, g1_ref, g2_ref, v0_ref, v1_ref, v2_ref, cwg_ref, cwv_ref, wd_ref, h_ref, lg_ref, lb_ref, o_ref):
    cwg, cwv = cwg_ref[...], cwv_ref[...]
    gate = cwg[0:1] * g0_ref[...] + cwg[1:2] * g1_ref[...] + cwg[2:3] * g2_ref[...]
    val = cwv[0:1] * v0_ref[...] + cwv[1:2] * v1_ref[...] + cwv[2:3] * v2_ref[...]
    _ffn_finish(gate, val, wd_ref, h_ref, lg_ref, lb_ref, o_ref)


def _sample_ffn(full, ffn_conv, wd_bf, h, ln_g, ln_b, Bd, Ts):
    blk = lambda c, r: pl.BlockSpec((Bd, D_FF), lambda t, c=c, r=r: (t + r, c))
    cw = lambda c: pl.BlockSpec((FFN_CONV_W, D_FF), lambda t, c=c: (0, c))
    const = lambda shape: pl.BlockSpec(shape, lambda t, n=len(shape): (0,) * n)
    xspec = pl.BlockSpec((Bd, D_MODEL), lambda t: (t, 0))
    return pl.pallas_call(
        _ffn_sample_body,
        grid=(Ts,),
        in_specs=[blk(0, 0), blk(0, 1), blk(0, 2), blk(1, 0), blk(1, 1), blk(1, 2), cw(0), cw(1),
                  const((D_FF, D_MODEL)), xspec, const((1, D_MODEL)), const((1, D_MODEL))],
        out_specs=xspec,
        out_shape=jax.ShapeDtypeStruct((Ts * Bd, D_MODEL), F32),
        compiler_params=_cp(("parallel",)),
        name="ffn_tail_sample",
    )(full, full, full, full, full, full, ffn_conv, ffn_conv, wd_bf, h, ln_g, ln_b)


def _sattn_body(pt_ref, q_ref, kn_ref, vn_ref, *rest, n_pages, pos0, ts):
    del pt_ref
    kp = rest[:n_pages]
    vp = rest[n_pages:2 * n_pages]
    o_ref = rest[2 * n_pages]
    qr = q_ref[0]
    nrow = qr.shape[0]
    ppb = MOBA_BLOCK // PAGE_SIZE
    n_past = n_pages // ppb
    blk_lane = lax.broadcasted_iota(jnp.int32, (BRANCH_W, LANES), 1)
    kmean_t = jnp.zeros((BRANCH_W, LANES), F32)
    for n in range(n_past):
        acc = kp[n * ppb][...]
        for r in range(1, ppb):
            acc = acc + kp[n * ppb + r][...]
        kmean_t = jnp.where(blk_lane == n, jnp.sum(acc, axis=1, keepdims=True) * (1.0 / MOBA_BLOCK), kmean_t)
    gate = jnp.dot(qr, kmean_t, precision=lax.Precision.HIGHEST, preferred_element_type=F32)
    bias = _top_blocks_bias(gate, n_past)
    qb = (qr * (A_DH ** -0.5)).astype(BF)
    scores = []
    for p in range(n_pages):
        s = jnp.dot(qb, kp[p][...].astype(BF), preferred_element_type=F32)
        scores.append(s + bias[:, p // ppb:p // ppb + 1])
    pad = jnp.zeros((LANES - kn_ref.shape[1], BRANCH_W), F32)
    kn = jnp.concatenate([kn_ref[0], pad], axis=0).astype(BF)
    vn = jnp.concatenate([vn_ref[0], pad], axis=0).astype(BF)
    s_own = lax.dot_general(qb, kn, NT, preferred_element_type=F32)
    qi = lax.broadcasted_iota(jnp.int32, (nrow, LANES), 0) // A_HEADS
    kj = lax.broadcasted_iota(jnp.int32, (nrow, LANES), 1)
    own_blk = pos0 // MOBA_BLOCK
    ok = (kj < ts) & (own_blk * MOBA_BLOCK + kj <= pos0 + qi)
    s_own = jnp.where(ok, s_own, NEG_INF)
    m = jnp.max(s_own, axis=1, keepdims=True)
    for s in scores:
        m = jnp.maximum(m, jnp.max(s, axis=1, keepdims=True))
    p_own = jnp.exp(s_own - m)
    l = jnp.sum(p_own, axis=1, keepdims=True)
    out = jnp.dot(p_own.astype(BF), vn, preferred_element_type=F32)
    for p in range(n_pages):
        pr = jnp.exp(scores[p] - m)
        l = l + jnp.sum(pr, axis=1, keepdims=True)
        out = out + lax.dot_general(pr.astype(BF), vp[p][...].astype(BF), NT, preferred_element_type=F32)
    out = out / l
    hrow = lax.broadcasted_iota(jnp.int32, (nrow, BRANCH_W), 0) % A_HEADS
    hlane = lax.broadcasted_iota(jnp.int32, (nrow, BRANCH_W), 1) // A_DH
    out = jnp.where(hrow == hlane, out, 0.0)
    o_ref[0] = jnp.sum(out.reshape(ts, A_HEADS, BRANCH_W), axis=1)


def _sample_attention(qrows, kn, vn, cache_k, cache_v, page_table, layer, pos0, ts):
    Bd, n_pages = page_table.shape
    nrow = qrows.shape[1]
    page = lambda p: pl.BlockSpec((None, None, BRANCH_W, PAGE_SIZE),
                                  lambda b, pt, p=p: (layer, pt[b * n_pages + p], 0, 0))
    req = lambda r: pl.BlockSpec((1, r, BRANCH_W), lambda b, pt: (b, 0, 0))
    grid_spec = pltpu.PrefetchScalarGridSpec(
        num_scalar_prefetch=1,
        grid=(Bd,),
        in_specs=[req(nrow), req(kn.shape[1]), req(vn.shape[1])] + [page(p) for p in range(n_pages)] * 2,
        out_specs=req(ts),
    )
    return pl.pallas_call(
        functools.partial(_sattn_body, n_pages=n_pages, pos0=pos0, ts=ts),
        grid_spec=grid_spec,
        out_shape=jax.ShapeDtypeStruct((Bd, ts, BRANCH_W), F32),
        compiler_params=_cp(("parallel",)),
        name="moba_sample",
    )(page_table.reshape(-1), qrows, kn, vn, *([cache_k] * n_pages), *([cache_v] * n_pages))


def _smix_body(uf_ref, db_ref, dc_ref, dh_ref, zb_ref, pw_ref, ps_ref, sw_ref, yb_ref, yd_ref, zn_ref, *, bd, ts, pos0):
    ps, sw = ps_ref[...], sw_ref[...]
    slab = lambda ref, r: ref[r * bd:(r + 1) * bd, :]
    z = [slab(zb_ref, r) for r in range(SCONV_W - 1)] + [slab(dc_ref, t) * slab(dh_ref, t) for t in range(ts)]
    for t in range(ts):
        u = slab(uf_ref, POOL_BUF + t)
        for g, w in enumerate(POOL_WINDOWS):
            sl = slice(g * POOL_GC, (g + 1) * POOL_GC)
            win = u[:, sl]
            for r in range(1, w):
                win = win + slab(uf_ref, POOL_BUF + t - r)[:, sl]
            pooled = win / float(min(w, pos0 + t + 1)) - u[:, sl]
            y = jnp.dot(pooled.astype(BF), pw_ref[g], preferred_element_type=F32) * ps[:, sl]
            yb_ref[t * bd:(t + 1) * bd, sl] = y.astype(yb_ref.dtype)
        conv = sw[0:1] * z[t] + sw[1:2] * z[t + 1] + sw[2:3] * z[t + 2]
        yd_ref[t * bd:(t + 1) * bd, :] = (slab(db_ref, t) * conv).astype(yd_ref.dtype)
    for r in range(SCONV_W - 1):
        zn_ref[r * bd:(r + 1) * bd, :] = z[ts + r]


def _sample_mixers(ufull, proj, zbuf, pool_w_bf, pool_scale, sconv_w, Bd, Ts, pos0):
    M = Ts * Bd
    whole = lambda a: pl.BlockSpec(a.shape, lambda i, n=a.ndim: (0,) * n)
    blk = lambda cb: pl.BlockSpec((M, BRANCH_W), lambda i, cb=cb: (0, cb))
    out = lambda r, dt: (pl.BlockSpec((r, BRANCH_W), lambda i: (0, 0)), jax.ShapeDtypeStruct((r, BRANCH_W), dt))
    outs = [out(M, BF), out(M, BF), out((SCONV_W - 1) * Bd, F32)]
    return pl.pallas_call(
        functools.partial(_smix_body, bd=Bd, ts=Ts, pos0=pos0),
        grid=(1,),
        in_specs=[whole(ufull), blk(CB_DB), blk(CB_DC), blk(CB_DH), whole(zbuf), whole(pool_w_bf), whole(pool_scale),
                  whole(sconv_w)],
        out_specs=[o[0] for o in outs],
        out_shape=[o[1] for o in outs],
        compiler_params=_cp(("arbitrary",)),
        name="pool_sconv_sample",
    )(ufull, proj, proj, proj, zbuf, pool_w_bf, pool_scale, sconv_w)


def _shgrn_body(cq_ref, cf_ref, ci_ref, cg_ref, s0_ref, lb_ref, ng_ref, yc_ref, so_ref, *, ts):
    lb, ng = lb_ref[...], ng_ref[...]
    for h in range(C_HEADS):
        sl = slice(h * C_DK, (h + 1) * C_DK)
        o, st = _gla_chunk(cq_ref[0, :, sl], cf_ref[0, :, sl], ci_ref[0, :, sl], lb[:, sl], s0_ref[0, h].T, ts)
        so_ref[0, h] = st.T
        yc_ref[0, :, sl] = _hgrn_out(o, cg_ref[0, :, sl], ng[:, sl])


def _sample_hgrn(cq, cf, ci, cg, s0, lb, ng, ts):
    Bd, rows, _ = cq.shape
    req = pl.BlockSpec((1, rows, BRANCH_W), lambda b: (b, 0, 0))
    st = pl.BlockSpec((1, C_HEADS, C_DK, C_DK), lambda b: (b, 0, 0, 0))
    vec = pl.BlockSpec((1, BRANCH_W), lambda b: (0, 0))
    return pl.pallas_call(
        functools.partial(_shgrn_body, ts=ts),
        grid=(Bd,),
        in_specs=[req, req, req, req, st, vec, vec],
        out_specs=[req, st],
        out_shape=[jax.ShapeDtypeStruct((Bd, rows, BRANCH_W), F32), jax.ShapeDtypeStruct(s0.shape, F32)],
        compiler_params=_cp(("parallel",)),
        name="hgrn_sample",
    )(cq, cf, ci, cg, s0, lb, ng)


def _prompt_layer(x, B, T, w, lb, tables):
    tm = min(T, 1024)
    proj = _matmul(x, w["w_in"], tm, BRANCH_W, "proj_prompt")
    kf, kb, qt, vt, bias = _attn_prep(proj, tables, B, T)
    ya = _prompt_attention(qt, kb, vt, bias, B, T)
    yb, yd, ztail = _prompt_mixers(proj, w["pool_w"], w["pool_scale"], w["sconv_w"], B, T, min(T, 512))
    yc, s_new = _prompt_hgrn(proj, lb, w["hgrn_norm_g"], B, T, min(T, 512))
    h = _merge((ya, yb, yc, yd), proj, x, w["w_branch"], w["w_o"], w["ln1_g"], w["ln1_b"], 256)
    up = _matmul(h, w["ffn_up"], tm, BRANCH_W, "ffn_up_prompt")
    out = _prompt_ffn(up, w["ffn_conv"], w["ffn_down"], h, w["ln2_g"], w["ln2_b"], B, T, 256)
    proj3 = proj.reshape(B, T, N_IN)
    k = kf.reshape(B, T, A_HEADS, A_DH)
    v = proj3[:, :, CB_AV * BRANCH_W:(CB_AV + 1) * BRANCH_W].reshape(B, T, A_HEADS, A_DH)
    new_pool = proj3[:, T - POOL_BUF:, CB_PU * BRANCH_W:(CB_PU + 1) * BRANCH_W]
    new_sconv = ztail[:, 8 - (SCONV_W - 1):]
    new_ffn = up.reshape(B, T, 2 * D_FF)[:, T - (FFN_CONV_W - 1):]
    return out, k, v, new_pool, s_new, new_sconv, new_ffn


def _to_tm(a):
    return a.transpose(1, 0, 2).reshape(a.shape[1] * a.shape[0], a.shape[2])


def _from_tm(a, Bd):
    return a.reshape(a.shape[0] // Bd, Bd, a.shape[1]).transpose(1, 0, 2)


def _sample_layer(x, Bd, Ts, pos0, layer, w, lb, tables, cache_k, cache_v, page_table, pool_buf, hgrn_s, sconv_buf, ffn_buf):
    M = Ts * Bd
    proj = _matmul(x, w["w_in"], M, BRANCH_W, "proj_sample")
    qf, _, kf, _, _ = _rope(proj, tables, M)
    col = lambda cb: proj[:, cb * BRANCH_W:(cb + 1) * BRANCH_W]
    q_r, k_r, v_r = _from_tm(qf, Bd), _from_tm(kf, Bd), _from_tm(col(CB_AV), Bd)
    head_of_lane = jnp.arange(BRANCH_W) // A_DH
    hmask = (head_of_lane[None, :] == jnp.arange(A_HEADS)[:, None]).astype(F32)
    qrows = (q_r[:, :, None, :] * hmask[None, None]).reshape(Bd, Ts * A_HEADS, BRANCH_W)
    pad8 = lambda a: jnp.pad(a, ((0, 0), (0, 8 - Ts), (0, 0)))
    ya = _sample_attention(qrows, pad8(k_r), pad8(v_r), cache_k, cache_v, page_table, layer, pos0, Ts)
    ufull = jnp.concatenate([_to_tm(pool_buf), col(CB_PU)], axis=0)
    yb, yd, znew = _sample_mixers(ufull, proj, _to_tm(sconv_buf), w["pool_w"], w["pool_scale"], w["sconv_w"], Bd, Ts, pos0)
    req8 = lambda cb: pad8(_from_tm(col(cb), Bd))
    yc, s_new = _sample_hgrn(req8(CB_CQ), req8(CB_CF), req8(CB_CI), req8(CB_CG), hgrn_s, lb, w["hgrn_norm_g"], Ts)
    ys = (_to_tm(ya), yb, _to_tm(yc[:, :Ts]), yd)
    h = _merge(ys, proj, x, w["w_branch"], w["w_o"], w["ln1_g"], w["ln1_b"], min(M, 256))
    up = _matmul(h, w["ffn_up"], M, BRANCH_W, "ffn_up_sample")
    full = jnp.concatenate([_to_tm(ffn_buf), up], axis=0)
    out = _sample_ffn(full, w["ffn_conv"], w["ffn_down"], h, w["ln2_g"], w["ln2_b"], Bd, Ts)
    k = k_r.reshape(Bd, Ts, A_HEADS, A_DH)
    v = v_r.reshape(Bd, Ts, A_HEADS, A_DH)
    new_pool = _from_tm(ufull[Ts * Bd:], Bd)
    new_sconv = _from_tm(znew, Bd)
    new_ffn = _from_tm(full[Ts * Bd:], Bd)
    return out, k, v, new_pool, s_new, new_sconv, new_ffn


def _layer_weights(l, w_in, w_branch, w_o, pool_w, pool_scale, hgrn_norm_g, sconv_w, ln1_g, ln1_b, ffn_up, ffn_conv,
                   ffn_down, ln2_g, ln2_b):
    row = lambda a: a[l][None, :]
    return {
        "w_in": w_in[l].astype(BF), "w_branch": w_branch[l].astype(BF), "w_o": w_o[l].astype(BF),
        "pool_w": pool_w[l].astype(BF), "pool_scale": row(pool_scale), "hgrn_norm_g": row(hgrn_norm_g),
        "sconv_w": sconv_w[l], "ln1_g": row(ln1_g), "ln1_b": row(ln1_b), "ffn_up": ffn_up[l].astype(BF),
        "ffn_conv": ffn_conv[l], "ffn_down": ffn_down[l].astype(BF), "ln2_g": row(ln2_g), "ln2_b": row(ln2_b),
    }


def kernel(x_prompt, x_sample, cache_k, cache_v, state_pool, state_hgrn, state_sconv, state_ffn, page_table, w_in,
           w_branch, w_o, pool_w, pool_scale, hgrn_lb_logits, hgrn_norm_g, sconv_w, ln1_g, ln1_b, ffn_up, ffn_conv,
           ffn_down, ln2_g, ln2_b):
    depth = w_in.shape[0]
    Bp, T, _ = x_prompt.shape
    Bd, Ts, _ = x_sample.shape
    n_pages = page_table.shape[1]
    pos0 = n_pages * PAGE_SIZE
    assert T % MOBA_BLOCK == 0 and pos0 % MOBA_BLOCK == 0 and Ts <= 8 and T >= HALO
    lb_all = _lower_bounds(hgrn_lb_logits.astype(F32))
    tab_p = _rope_tables(jnp.arange(T))
    tab_s = _rope_tables(pos0 + jnp.repeat(jnp.arange(Ts), Bd))
    ck = cache_k.transpose(0, 1, 3, 4, 2).reshape(depth, cache_k.shape[1], BRANCH_W, PAGE_SIZE)
    cv = cache_v.transpose(0, 1, 3, 4, 2).reshape(depth, cache_v.shape[1], BRANCH_W, PAGE_SIZE)
    hp = x_prompt.reshape(Bp * T, D_MODEL)
    hs = _to_tm(x_sample)
    outs_p, outs_s = [], []
    for l in range(depth):
        w = _layer_weights(l, w_in, w_branch, w_o, pool_w, pool_scale, hgrn_norm_g, sconv_w, ln1_g, ln1_b, ffn_up,
                           ffn_conv, ffn_down, ln2_g, ln2_b)
        lb = lb_all[l][None, :]
        rp = _prompt_layer(hp, Bp, T, w, lb, tab_p)
        rs = _sample_layer(hs, Bd, Ts, pos0, l, w, lb, tab_s, ck, cv, page_table, state_pool[l], state_hgrn[l],
                           state_sconv[l], state_ffn[l])
        hp, hs = rp[0], rs[0]
        outs_p.append(rp[1:])
        outs_s.append(rs[1:])
    stack = lambda outs: tuple(jnp.stack([o[j] for o in outs]) for j in range(6))
    return (hp.reshape(Bp, T, D_MODEL), _from_tm(hs, Bd)) + stack(outs_p) + stack(outs_s)
```

```python
import functools
import math

import jax
import jax.numpy as jnp
import numpy as np
from jax import lax
from jax.experimental import pallas as pl
from jax.experimental.pallas import tpu as pltpu

F32 = jnp.float32
BF = jnp.bfloat16
NEG_INF = float("-inf")
LOG2_E = math.log2(math.e)

D_MODEL = 1024
BRANCH_W = 512
N_BRANCH = 4
A_HEADS = 8
A_DH = 64
MOBA_BLOCK = 256
MOBA_TOPK = 3
ROPE_DIM = A_DH // 4
ROPE_THETA = 500000.0
POOL_WINDOWS = (2, 4, 8, 16)
POOL_GC = 128
POOL_BUF = 15
C_HEADS = 4
C_DK = 128
SCONV_W = 3
D_FF = 2816
FFN_CONV_W = 3
PAGE_SIZE = 128
DEPTH = 4
DN_ALPHA = (2 * DEPTH) ** 0.25
LN_EPS = 1e-5
RMS_EPS = 1e-6

CB_AQ, CB_AK, CB_AV, CB_PU, CB_CQ, CB_CF, CB_CI, CB_CG, CB_DB, CB_DC, CB_DH, CB_GATE = range(12)
N_IN = (CB_GATE + 2 * N_BRANCH) * BRANCH_W

LANES = 128
HALO = 16
VMEM_LIMIT = 56 * 1024 * 1024

NT = (((1,), (1,)), ((), ()))
TN = (((0,), (0,)), ((), ()))


def _cp(sem, vmem=VMEM_LIMIT):
    return pltpu.CompilerParams(dimension_semantics=sem, vmem_limit_bytes=vmem)


def _sigmoid(x):
    return 1.0 / (1.0 + jnp.exp(-x))


def _silu(x):
    return x * _sigmoid(x)


def _layer_norm(x, g, b):
    mu = jnp.mean(x, axis=-1, keepdims=True)
    xc = x - mu
    var = jnp.mean(xc * xc, axis=-1, keepdims=True)
    return xc * lax.rsqrt(var + LN_EPS) * g + b


def _lb_body(logit_ref, o_ref):
    x = logit_ref[...]
    m = jnp.max(x, axis=0, keepdims=True)
    e = jnp.exp(x - m)
    p = e / jnp.sum(e, axis=0, keepdims=True)
    acc = jnp.zeros_like(p[0:1])
    rows = []
    for l in range(x.shape[0]):
        acc = acc + p[l:l + 1]
        rows.append(acc - p[0:1])
    o_ref[...] = jnp.concatenate(rows, axis=0)


def _lower_bounds(logits):
    return pl.pallas_call(_lb_body, out_shape=jax.ShapeDtypeStruct(logits.shape, F32), name="hgrn_lb")(logits)


def _mm_body(x_ref, w_ref, o_ref, xb_ref):
    @pl.when(pl.program_id(1) == 0)
    def _():
        xb_ref[...] = x_ref[...].astype(BF)

    o_ref[...] = jnp.dot(xb_ref[...], w_ref[...], preferred_element_type=F32)


def _matmul(x, w_bf, tm, tn, name):
    M, K = x.shape
    N = w_bf.shape[1]
    assert M % tm == 0 and N % tn == 0
    return pl.pallas_call(
        _mm_body,
        grid=(M // tm, N // tn),
        in_specs=[pl.BlockSpec((tm, K), lambda i, j: (i, 0)),
                  pl.BlockSpec((K, tn), lambda i, j: (0, j))],
        out_specs=pl.BlockSpec((tm, tn), lambda i, j: (i, j)),
        out_shape=jax.ShapeDtypeStruct((M, N), F32),
        scratch_shapes=[pltpu.VMEM((tm, K), BF)],
        compiler_params=_cp(("parallel", "arbitrary")),
        name=name,
    )(x, w_bf)


def _rope_tables(pos):
    half = ROPE_DIM // 2
    inv = jnp.power(ROPE_THETA, -jnp.arange(0, ROPE_DIM, 2, dtype=F32) / ROPE_DIM)
    ang = pos.astype(F32)[:, None] * inv[None, :]
    cos, sin = jnp.cos(ang), jnp.sin(ang)
    n = pos.shape[0]
    ones = jnp.ones((n, A_DH - ROPE_DIM), F32)
    zeros = jnp.zeros((n, A_DH - ROPE_DIM), F32)
    zh = jnp.zeros((n, half), F32)
    c = jnp.concatenate([cos, cos, ones], axis=1)
    s_lo = jnp.concatenate([-sin, zh, zeros], axis=1)
    s_hi = jnp.concatenate([zh, sin, zeros], axis=1)
    tile = lambda a: jnp.tile(a, (1, A_HEADS))
    return tile(c), tile(s_lo), tile(s_hi)


def _rope_body(q_ref, k_ref, v_ref, c_ref, sl_ref, sh_ref, qf_ref, qb_ref, kf_ref, kb_ref, vb_ref):
    c, sl, sh = c_ref[...], sl_ref[...], sh_ref[...]
    half = ROPE_DIM // 2
    w = q_ref.shape[1]

    def rope(x):
        return x * c + pltpu.roll(x, w - half, 1) * sl + pltpu.roll(x, half, 1) * sh

    q = rope(q_ref[...])
    k = rope(k_ref[...])
    qf_ref[...] = q
    qb_ref[...] = (q * (A_DH ** -0.5)).astype(BF)
    kf_ref[...] = k
    kb_ref[...] = k.astype(BF)
    vb_ref[...] = v_ref[...].astype(BF)


def _rope(proj, tables, tm):
    M = proj.shape[0]
    ntab = tables[0].shape[0] // tm
    blk = lambda cb: pl.BlockSpec((tm, BRANCH_W), lambda i, cb=cb: (i, cb))
    tab = pl.BlockSpec((tm, BRANCH_W), lambda i: (i % ntab, 0))
    out = pl.BlockSpec((tm, BRANCH_W), lambda i: (i, 0))
    sds = lambda dt: jax.ShapeDtypeStruct((M, BRANCH_W), dt)
    return pl.pallas_call(
        _rope_body,
        grid=(M // tm,),
        in_specs=[blk(CB_AQ), blk(CB_AK), blk(CB_AV), tab, tab, tab],
        out_specs=[out] * 5,
        out_shape=[sds(F32), sds(BF), sds(F32), sds(BF), sds(BF)],
        compiler_params=_cp(("parallel",)),
        name="rope",
    )(proj, proj, proj, *tables)


def _top_blocks_bias(gate, n_past):
    col = lax.broadcasted_iota(jnp.int32, gate.shape, 1)
    g = jnp.where(col < n_past, gate, NEG_INF)
    bias = jnp.full(gate.shape, NEG_INF, F32)
    for _ in range(MOBA_TOPK):
        m = jnp.max(g, axis=1, keepdims=True)
        idx = jnp.min(jnp.where(g == m, col, LANES), axis=1, keepdims=True)
        pick = (col == idx) & (m > NEG_INF)
        bias = jnp.where(pick, 0.0, bias)
        g = jnp.where(pick, NEG_INF, g)
    return bias


def _split_bf16(x):
    hi = x.astype(BF)
    return hi, (x - hi.astype(F32)).astype(BF)


def _prep_body(q_ref, k_ref, v_ref, c_ref, sl_ref, sh_ref, kt_ref, vtf_ref, kb_ref, qt_ref, vt_ref, bias_ref, km_ref):
    i = pl.program_id(1)
    nbp = km_ref.shape[0]

    @pl.when(i == 0)
    def _():
        km_ref[...] = jnp.zeros_like(km_ref)

    c, sl, sh = c_ref[...], sl_ref[...], sh_ref[...]
    half = ROPE_DIM // 2

    def rope(x):
        return x * c + pltpu.roll(x, BRANCH_W - half, 1) * sl + pltpu.roll(x, half, 1) * sh

    q = rope(q_ref[...])
    k = rope(k_ref[...])
    kt_ref[...] = k.T
    kb_ref[...] = k.astype(BF)
    qt_ref[...] = (q * (A_DH ** -0.5 * LOG2_E)).T.astype(BF)
    v_t = v_ref[...].T
    vtf_ref[...] = v_t
    vt_ref[...] = v_t.astype(BF)

    km = km_ref[...]
    lane = lax.broadcasted_iota(jnp.int32, km.shape, 1)
    stack = jnp.concatenate([jnp.where(lane // A_DH == h, km, 0.0) for h in range(A_HEADS)], axis=0)
    k_hi, k_lo = _split_bf16(stack)
    q_hi, q_lo = _split_bf16(q)
    dot = lambda a, b: lax.dot_general(a, b, NT, preferred_element_type=F32)
    gate = (dot(k_hi, q_hi) + dot(k_lo, q_hi) + dot(k_hi, q_lo)).reshape(A_HEADS, nbp, MOBA_BLOCK)
    blk = lax.broadcasted_iota(jnp.int32, gate.shape, 1)
    gate = jnp.where(blk < i, gate, NEG_INF)
    bias = jnp.full(gate.shape, NEG_INF, F32)
    for _ in range(MOBA_TOPK):
        m = jnp.max(gate, axis=1, keepdims=True)
        idx = jnp.min(jnp.where(gate == m, blk, nbp), axis=1, keepdims=True)
        pick = (blk == idx) & (m > NEG_INF)
        bias = jnp.where(pick, 0.0, bias)
        gate = jnp.where(pick, NEG_INF, gate)
    bias_ref[0] = bias.reshape(A_HEADS * nbp, MOBA_BLOCK)
    km_ref[pl.ds(i, 1), :] = jnp.sum(k, axis=0, keepdims=True) * (1.0 / MOBA_BLOCK)


def _attn_prep(proj, tables, B, T):
    nq = T // MOBA_BLOCK
    nbp = -(-nq // 8) * 8
    tq = MOBA_BLOCK
    blk = lambda cb: pl.BlockSpec((tq, BRANCH_W), lambda b, i, cb=cb: (b * nq + i, cb))
    tab = pl.BlockSpec((tq, BRANCH_W), lambda b, i: (i, 0))
    row = pl.BlockSpec((tq, BRANCH_W), lambda b, i: (b * nq + i, 0))
    tr = pl.BlockSpec((BRANCH_W, tq), lambda b, i: (b, i))
    return pl.pallas_call(
        _prep_body,
        grid=(B, nq),
        in_specs=[blk(CB_AQ), blk(CB_AK), blk(CB_AV), tab, tab, tab],
        out_specs=[tr, tr, row, tr, tr, pl.BlockSpec((1, A_HEADS * nbp, tq), lambda b, i: (b, 0, i))],
        out_shape=[jax.ShapeDtypeStruct((B * BRANCH_W, T), F32), jax.ShapeDtypeStruct((B * BRANCH_W, T), F32),
                   jax.ShapeDtypeStruct((B * T, BRANCH_W), BF),
                   jax.ShapeDtypeStruct((B * BRANCH_W, T), BF), jax.ShapeDtypeStruct((B * BRANCH_W, T), BF),
                   jax.ShapeDtypeStruct((B, A_HEADS * nbp, T), F32)],
        scratch_shapes=[pltpu.VMEM((nbp, BRANCH_W), F32)],
        compiler_params=_cp(("parallel", "arbitrary")),
        name="attn_prep",
    )(proj, proj, proj, *tables)


def _attn_body(qt_ref, k_ref, vt_ref, bias_ref, o_ref, sa_ref, sb_ref):
    i = pl.program_id(2)
    tq = MOBA_BLOCK
    npair = LANES // A_DH
    nbp = bias_ref.shape[1] // npair
    qt = qt_ref[...]
    rowi = lax.broadcasted_iota(jnp.int32, qt.shape, 0)
    qts = [jnp.where((rowi >= e * A_DH) & (rowi < (e + 1) * A_DH), qt, jnp.zeros_like(qt)) for e in range(npair)]
    block = lambda j: pl.ds(j * tq if isinstance(j, int) else pl.multiple_of(j * tq, tq), tq)

    def scores(s_ref, j):
        kj = k_ref[block(j), :]
        for e in range(npair):
            s_ref[e] = jnp.dot(kj, qts[e], preferred_element_type=F32)

    def consume(s_ref, j, carry, own):
        new = []
        for e in range(npair):
            m, l, acc = carry[e]
            s = s_ref[e]
            if own:
                causal = (lax.broadcasted_iota(jnp.int32, (tq, tq), 0) <= lax.broadcasted_iota(jnp.int32, (tq, tq), 1))
                s = jnp.where(causal, s, NEG_INF)
                mn = jnp.maximum(m, jnp.max(s, axis=0, keepdims=True))
                shift = mn
            else:
                b = bias_ref[0, pl.ds(e * nbp + j, 1), :]
                mn = jnp.maximum(m, jnp.max(s, axis=0, keepdims=True) + b)
                shift = mn - b
            a = jnp.exp2(m - mn)
            p = jnp.exp2(s - shift)
            l = a * l + jnp.sum(p, axis=0, keepdims=True)
            acc = a * acc + jnp.dot(vt_ref[e * A_DH:(e + 1) * A_DH, block(j)], p.astype(BF), preferred_element_type=F32)
            new.append((mn, l, acc))
        return tuple(new)

    def two_blocks(p, carry):
        scores(sb_ref, 2 * p + 1)
        carry = consume(sa_ref, 2 * p, carry, own=False)
        scores(sa_ref, 2 * p + 2)
        return consume(sb_ref, 2 * p + 1, carry, own=False)

    def odd_tail(carry):
        scores(sb_ref, i)
        carry = consume(sa_ref, i - 1, carry, own=False)
        return consume(sb_ref, i, carry, own=True)

    def even_tail(carry):
        return consume(sa_ref, i, carry, own=True)

    init = tuple((jnp.full((1, tq), -1e30, F32), jnp.zeros((1, tq), F32), jnp.zeros((A_DH, tq), F32))
                 for _ in range(npair))
    scores(sa_ref, 0)
    state = lax.fori_loop(0, i // 2, two_blocks, init)
    fin = lax.cond(i % 2 == 1, odd_tail, even_tail, state)
    out_t = jnp.concatenate([acc / l for (_, l, acc) in fin], axis=0)
    o_ref[...] = out_t.T.astype(o_ref.dtype)


def _prompt_attention(qt, kb, vt, bias, B, T):
    nq = T // MOBA_BLOCK
    npair = LANES // A_DH
    ngrp = BRANCH_W // LANES
    nbp = bias.shape[1] // A_HEADS
    return pl.pallas_call(
        _attn_body,
        grid=(B, ngrp, nq),
        in_specs=[pl.BlockSpec((LANES, MOBA_BLOCK), lambda b, hp, i: (b * ngrp + hp, i)),
                  pl.BlockSpec((T, LANES), lambda b, hp, i: (b, hp)),
                  pl.BlockSpec((LANES, T), lambda b, hp, i: (b * ngrp + hp, 0)),
                  pl.BlockSpec((1, npair * nbp, MOBA_BLOCK), lambda b, hp, i: (b, hp, i))],
        out_specs=pl.BlockSpec((MOBA_BLOCK, LANES), lambda b, hp, i: (b * nq + i, hp)),
        out_shape=jax.ShapeDtypeStruct((B * T, BRANCH_W), BF),
        scratch_shapes=[pltpu.VMEM((npair, MOBA_BLOCK, MOBA_BLOCK), F32)] * 2,
        compiler_params=_cp(("parallel", "parallel", "arbitrary")),
        name="moba_prompt",
    )(qt, kb, vt, bias)


def _pool_group(ext_ref, u, g, w, t_first, pw, ps):
    tm = u.shape[0]
    sl = slice(g * POOL_GC, (g + 1) * POOL_GC)
    win = ext_ref[HALO:HALO + tm, sl]
    for r in range(1, w):
        win = win + ext_ref[HALO - r:HALO - r + tm, sl]
    t = t_first + lax.broadcasted_iota(jnp.int32, (tm, POOL_GC), 0)
    cnt = jnp.minimum(w, t + 1).astype(F32)
    pooled = win / cnt - u[:, sl]
    return jnp.dot(pooled.astype(BF), pw, preferred_element_type=F32) * ps[:, sl]


def _mix_body(u_ref, uh_ref, db_ref, dc_ref, dh_ref, dch_ref, dhh_ref, pw_ref, ps_ref, sw_ref,
              yb_ref, yd_ref, zt_ref, ext_ref, zext_ref):
    i = pl.program_id(1)
    tm = u_ref.shape[0]
    first = i == 0
    u = u_ref[...]
    ext_ref[0:HALO, :] = jnp.where(first, 0.0, uh_ref[...])
    ext_ref[HALO:, :] = u
    ps = ps_ref[...]
    for g, w in enumerate(POOL_WINDOWS):
        y = _pool_group(ext_ref, u, g, w, i * tm, pw_ref[g], ps)
        yb_ref[:, g * POOL_GC:(g + 1) * POOL_GC] = y.astype(yb_ref.dtype)
    z = dc_ref[...] * dh_ref[...]
    zext_ref[0:8, :] = jnp.where(first, 0.0, dch_ref[...] * dhh_ref[...])
    zext_ref[8:, :] = z
    sw = sw_ref[...]
    conv = sw[0:1] * zext_ref[6:6 + tm, :] + sw[1:2] * zext_ref[7:7 + tm, :] + sw[2:3] * z
    yd_ref[...] = (db_ref[...] * conv).astype(yd_ref.dtype)
    zt_ref[0] = z[tm - 8:tm]


def _prompt_mixers(proj, pool_w_bf, pool_scale, sconv_w, B, T, tm):
    nt = T // tm
    row = lambda b, i: b * nt + i
    blk = lambda cb: pl.BlockSpec((tm, BRANCH_W), lambda b, i, cb=cb: (row(b, i), cb))

    def halo(cb, h):
        return pl.BlockSpec((h, BRANCH_W), lambda b, i, cb=cb, h=h: (jnp.maximum(row(b, i) * (tm // h) - 1, 0), cb))

    const = lambda shape: pl.BlockSpec(shape, lambda b, i, n=len(shape): (0,) * n)
    return pl.pallas_call(
        _mix_body,
        grid=(B, nt),
        in_specs=[blk(CB_PU), halo(CB_PU, HALO), blk(CB_DB), blk(CB_DC), blk(CB_DH), halo(CB_DC, 8), halo(CB_DH, 8),
                  const((len(POOL_WINDOWS), POOL_GC, POOL_GC)), const((1, BRANCH_W)), const((SCONV_W, BRANCH_W))],
        out_specs=[pl.BlockSpec((tm, BRANCH_W), lambda b, i: (row(b, i), 0)),
                   pl.BlockSpec((tm, BRANCH_W), lambda b, i: (row(b, i), 0)),
                   pl.BlockSpec((1, 8, BRANCH_W), lambda b, i: (b, 0, 0))],
        out_shape=[jax.ShapeDtypeStruct((B * T, BRANCH_W), BF), jax.ShapeDtypeStruct((B * T, BRANCH_W), BF),
                   jax.ShapeDtypeStruct((B, 8, BRANCH_W), F32)],
        scratch_shapes=[pltpu.VMEM((tm + HALO, BRANCH_W), F32), pltpu.VMEM((tm + 8, BRANCH_W), F32)],
        compiler_params=_cp(("parallel", "arbitrary")),
        name="pool_sconv_prompt",
    )(proj, proj, proj, proj, proj, proj, proj, pool_w_bf, pool_scale, sconv_w)


def _gla_chunk(cq, cf, ci, lb, st, n_valid):
    rows = cq.shape[0]
    f = lb + (1.0 - lb) * _sigmoid(cf)
    g = jnp.log(f)
    r_i = lax.broadcasted_iota(jnp.int32, (rows, rows), 0)
    c_i = lax.broadcasted_iota(jnp.int32, (rows, rows), 1)
    tri = jnp.where((r_i >= c_i) & (c_i < n_valid), 1.0, 0.0).astype(F32)
    b = jnp.dot(tri, g, precision=lax.Precision.HIGHEST, preferred_element_type=F32)
    qs = _silu(cq)
    kk = 1.0 - f
    t_i = lax.broadcasted_iota(jnp.int32, (rows, 1), 0)
    o = jnp.zeros((rows, C_DK), F32)
    for s in range(n_valid):
        e = jnp.exp(jnp.minimum(b - b[s:s + 1], 0.0))
        a = jnp.sum(qs * kk[s:s + 1] * e, axis=1, keepdims=True)
        o = o + jnp.where(t_i >= s, a, 0.0) * ci[s:s + 1]
    o = o + lax.dot_general((qs * jnp.exp(b)).astype(BF), st.astype(BF), NT, preferred_element_type=F32)
    bl = b[n_valid - 1:n_valid]
    ks = jnp.where(t_i < n_valid, kk * jnp.exp(jnp.minimum(bl - b, 0.0)), 0.0)
    upd = lax.dot_general(ci.astype(BF), ks.astype(BF), TN, preferred_element_type=F32)
    return o, st * jnp.exp(bl) + upd


def _hgrn_out(o, cg, ng):
    o = o * lax.rsqrt(jnp.mean(o * o, axis=1, keepdims=True) + RMS_EPS)
    return o * ng * _silu(cg)


GROUP = 128
GROUP_HALVES = tuple(GROUP >> (s + 1) for s in range(int(math.log2(GROUP))))


def _decay_sum_matrix():
    t = np.arange(GROUP)[:, None]
    u = np.arange(GROUP)[None, :]
    blocks = [u <= t, u > t]
    for h in GROUP_HALVES:
        r = (t // (2 * h)) * 2 * h + h - 1
        second = (t // h) % 2 == 1
        blocks.append(np.where(second, (u > r) & (u <= t), (u > t) & (u <= r)))
    d = np.concatenate(blocks, axis=0).astype(np.float32)
    return jnp.asarray(np.concatenate([d, d], axis=1), dtype=BF)


def _gla_decays(cf, lb, dmat):
    f = lb + (1.0 - lb) * _sigmoid(cf)
    g_hi, g_lo = _split_bf16(jnp.log(f))
    return f, jnp.dot(dmat, jnp.concatenate([g_hi, g_lo], axis=0), preferred_element_type=F32)


def _gla_group(cq, f, x, ci, st):
    e = jnp.exp(x)
    blk = lambda n: e[n * GROUP:(n + 1) * GROUP]
    qs = _silu(cq)
    kk = 1.0 - f
    v = ci.astype(BF)
    t_i = lax.broadcasted_iota(jnp.int32, (GROUP, GROUP), 0)
    s_i = lax.broadcasted_iota(jnp.int32, (GROUP, GROUP), 1)
    row = lax.broadcasted_iota(jnp.int32, (GROUP, C_DK), 0)
    dot_nt = lambda a, b: lax.dot_general(a.astype(BF), b.astype(BF), NT, preferred_element_type=F32)
    attn = jnp.where(t_i == s_i, dot_nt(qs, kk), 0.0)
    for s, h in enumerate(GROUP_HALVES):
        second = (row // h) % 2 == 1
        scaled_q = jnp.where(second, qs * blk(2 + s), 0.0)
        scaled_k = jnp.where(second, 0.0, kk * blk(2 + s))
        part = dot_nt(scaled_q, scaled_k)
        attn = attn + (part if 2 * h == GROUP else jnp.where(t_i // (2 * h) == s_i // (2 * h), part, 0.0))
    o = jnp.dot(attn.astype(BF), v, preferred_element_type=F32) + dot_nt(qs * blk(0), st)
    upd = lax.dot_general(v, (kk * blk(1)).astype(BF), TN, preferred_element_type=F32)
    return o, st * e[GROUP - 1:GROUP] + upd


def _hgrn_body(cq_ref, cf_ref, ci_ref, cg_ref, lb_ref, ng_ref, dm_ref, yc_ref, so_ref, st_ref):
    it = pl.program_id(1)
    tc = cq_ref.shape[0]

    @pl.when(it == 0)
    def _():
        st_ref[...] = jnp.zeros_like(st_ref)

    lb, ng = lb_ref[...], ng_ref[...]

    def group(gi, carry):
        rows = pl.ds(pl.multiple_of(gi * GROUP, GROUP), GROUP)
        head = lambda h: slice(h * C_DK, (h + 1) * C_DK)
        f, x = _gla_decays(cf_ref[rows, :], lb, dm_ref[...])
        for h in range(C_HEADS):
            sl = head(h)
            o, st = _gla_group(cq_ref[rows, sl], f[:, sl], x[:, sl], ci_ref[rows, sl], st_ref[h])
            st_ref[h] = st
            yc_ref[rows, sl] = _hgrn_out(o, cg_ref[rows, sl], ng[:, sl]).astype(yc_ref.dtype)
        return carry

    lax.fori_loop(0, tc // GROUP, group, 0)

    @pl.when(it == pl.num_programs(1) - 1)
    def _():
        for h in range(C_HEADS):
            so_ref[0, h] = st_ref[h].T


def _prompt_hgrn(proj, lb, ng, B, T, tc):
    nt = T // tc
    dmat = _decay_sum_matrix()
    blk = lambda cb: pl.BlockSpec((tc, BRANCH_W), lambda b, i, cb=cb: (b * nt + i, cb))
    vec = pl.BlockSpec((1, BRANCH_W), lambda b, i: (0, 0))
    return pl.pallas_call(
        _hgrn_body,
        grid=(B, nt),
        in_specs=[blk(CB_CQ), blk(CB_CF), blk(CB_CI), blk(CB_CG), vec, vec,
                  pl.BlockSpec(dmat.shape, lambda b, i: (0, 0))],
        out_specs=[pl.BlockSpec((tc, BRANCH_W), lambda b, i: (b * nt + i, 0)),
                   pl.BlockSpec((1, C_HEADS, C_DK, C_DK), lambda b, i: (b, 0, 0, 0))],
        out_shape=[jax.ShapeDtypeStruct((B * T, BRANCH_W), BF), jax.ShapeDtypeStruct((B, C_HEADS, C_DK, C_DK), F32)],
        scratch_shapes=[pltpu.VMEM((C_HEADS, C_DK, C_DK), F32)],
        compiler_params=_cp(("parallel", "arbitrary")),
        name="hgrn_prompt",
    )(proj, proj, proj, proj, lb, ng, dmat)


def _merge_body(ya_ref, yb_ref, yc_ref, yd_ref, *rest):
    g_refs = rest[:2 * N_BRANCH]
    x_ref, wb_ref, wo_ref, lg_ref, lbias_ref, h_ref = rest[2 * N_BRANCH:]
    halves = [jnp.zeros((x_ref.shape[0], BRANCH_W), F32) for _ in range(2)]
    for n, y_ref in enumerate((ya_ref, yb_ref, yc_ref, yd_ref)):
        br = jnp.dot(y_ref[...].astype(BF), wb_ref[n], preferred_element_type=F32)
        for hh in range(2):
            gate = _sigmoid(g_refs[2 * n + hh][...])
            halves[hh] = halves[hh] + gate * br[:, hh * BRANCH_W:(hh + 1) * BRANCH_W]
    mixed = jnp.concatenate(halves, axis=1).astype(BF)
    pre = DN_ALPHA * x_ref[...] + jnp.dot(mixed, wo_ref[...], preferred_element_type=F32)
    h_ref[...] = _layer_norm(pre, lg_ref[...], lbias_ref[...])


def _merge(ys, proj, x, wb_bf, wo_bf, ln_g, ln_b, tm):
    M = x.shape[0]
    yspec = pl.BlockSpec((tm, BRANCH_W), lambda i: (i, 0))
    gspec = lambda c: pl.BlockSpec((tm, BRANCH_W), lambda i, c=c: (i, CB_GATE + c))
    xspec = pl.BlockSpec((tm, D_MODEL), lambda i: (i, 0))
    const = lambda shape: pl.BlockSpec(shape, lambda i, n=len(shape): (0,) * n)
    return pl.pallas_call(
        _merge_body,
        grid=(M // tm,),
        in_specs=[yspec] * N_BRANCH + [gspec(c) for c in range(2 * N_BRANCH)]
        + [xspec, const((N_BRANCH, BRANCH_W, D_MODEL)), const((D_MODEL, D_MODEL)), const((1, D_MODEL)), const((1, D_MODEL))],
        out_specs=xspec,
        out_shape=jax.ShapeDtypeStruct((M, D_MODEL), F32),
        compiler_params=_cp(("parallel",)),
        name="merge_ln1",
    )(*ys, *([proj] * (2 * N_BRANCH)), x, wb_bf, wo_bf, ln_g, ln_b)


def _ffn_finish(gate, val, wd_ref, h_ref, lg_ref, lb_ref, o_ref):
    a = (_silu(gate) * val).astype(BF)
    pre = DN_ALPHA * h_ref[...] + jnp.dot(a, wd_ref[...], preferred_element_type=F32)
    o_ref[...] = _layer_norm(pre, lg_ref[...], lb_ref[...])


FFN_COLS = 256


def _ffn_up_body(x_ref, w_ref, cw_ref, a_ref, tail_ref, halo_ref, *, tiles_per_seq):
    i = pl.program_id(0)
    tm = x_ref.shape[0]

    @pl.when(i == 0)
    def _():
        halo_ref[...] = jnp.zeros_like(halo_ref)

    xb = x_ref[...].astype(BF)
    seq_start = i % tiles_per_seq == 0

    def conv(cols):
        u = jnp.dot(xb, w_ref[:, cols], preferred_element_type=F32)
        prev = jnp.where(seq_start, 0.0, halo_ref[:, cols])
        halo_ref[:, cols] = u[tm - 8:tm]
        tail_ref[0, :, cols] = u[tm - 8:tm]
        cw = cw_ref[:, cols]
        back1 = jnp.concatenate([prev[7:8], u[:tm - 1]], axis=0)
        back2 = jnp.concatenate([prev[6:8], u[:tm - 2]], axis=0)
        return cw[0:1] * back2 + cw[1:2] * back1 + cw[2:3] * u

    for c in range(D_FF // FFN_COLS):
        gate = conv(slice(c * FFN_COLS, (c + 1) * FFN_COLS))
        val = conv(slice(D_FF + c * FFN_COLS, D_FF + (c + 1) * FFN_COLS))
        a_ref[:, c * FFN_COLS:(c + 1) * FFN_COLS] = (_silu(gate) * val).astype(a_ref.dtype)


def _prompt_ffn_up(h, wu_bf, ffn_conv, B, T, tm):
    M, K = h.shape
    nt = T // tm
    assert D_FF % FFN_COLS == 0
    const = lambda shape: pl.BlockSpec(shape, lambda i, n=len(shape): (0,) * n)
    return pl.pallas_call(
        functools.partial(_ffn_up_body, tiles_per_seq=nt),
        grid=(M // tm,),
        in_specs=[pl.BlockSpec((tm, K), lambda i: (i, 0)), const((K, 2 * D_FF)), const((FFN_CONV_W, 2 * D_FF))],
        out_specs=[pl.BlockSpec((tm, D_FF), lambda i: (i, 0)),
                   pl.BlockSpec((1, 8, 2 * D_FF), lambda i: (i // nt, 0, 0))],
        out_shape=[jax.ShapeDtypeStruct((M, D_FF), BF), jax.ShapeDtypeStruct((B, 8, 2 * D_FF), F32)],
        scratch_shapes=[pltpu.VMEM((8, 2 * D_FF), F32)],
        compiler_params=_cp(("arbitrary",)),
        name="ffn_up_act_prompt",
    )(h, wu_bf, ffn_conv)


def _ffn_down_body(a_ref, wd_ref, h_ref, lg_ref, lb_ref, o_ref):
    pre = DN_ALPHA * h_ref[...] + jnp.dot(a_ref[...], wd_ref[...], preferred_element_type=F32)
    o_ref[...] = _layer_norm(pre, lg_ref[...], lb_ref[...])


def _prompt_ffn_down(a, wd_bf, h, ln_g, ln_b, tm):
    M = h.shape[0]
    const = lambda shape: pl.BlockSpec(shape, lambda i, n=len(shape): (0,) * n)
    xspec = pl.BlockSpec((tm, D_MODEL), lambda i: (i, 0))
    return pl.pallas_call(
        _ffn_down_body,
        grid=(M // tm,),
        in_specs=[pl.BlockSpec((tm, D_FF), lambda i: (i, 0)), const((D_FF, D_MODEL)), xspec,
                  const((1, D_MODEL)), const((1, D_MODEL))],
        out_specs=xspec,
        out_shape=jax.ShapeDtypeStruct((M, D_MODEL), F32),
        compiler_params=_cp(("parallel",)),
        name="ffn_down_prompt",
    )(a, wd_bf, h, ln_g, ln_b)


def _ffn_sample_body(g0_ref, g1_ref, g2_ref, v0_ref, v1_ref, v2_ref, cwg_ref, cwv_ref, wd_ref, h_ref, lg_ref, lb_ref, o_ref):
    cwg, cwv = cwg_ref[...], cwv_ref[...]
    gate = cwg[0:1] * g0_ref[...] + cwg[1:2] * g1_ref[...] + cwg[2:3] * g2_ref[...]
    val = cwv[0:1] * v0_ref[...] + cwv[1:2] * v1_ref[...] + cwv[2:3] * v2_ref[...]
    _ffn_finish(gate, val, wd_ref, h_ref, lg_ref, lb_ref, o_ref)


def _sample_ffn(full, ffn_conv, wd_bf, h, ln_g, ln_b, Bd, Ts):
    blk = lambda c, r: pl.BlockSpec((Bd, D_FF), lambda t, c=c, r=r: (t + r, c))
    cw = lambda c: pl.BlockSpec((FFN_CONV_W, D_FF), lambda t, c=c: (0, c))
    const = lambda shape: pl.BlockSpec(shape, lambda t, n=len(shape): (0,) * n)
    xspec = pl.BlockSpec((Bd, D_MODEL), lambda t: (t, 0))
    return pl.pallas_call(
        _ffn_sample_body,
        grid=(Ts,),
        in_specs=[blk(0, 0), blk(0, 1), blk(0, 2), blk(1, 0), blk(1, 1), blk(1, 2), cw(0), cw(1),
                  const((D_FF, D_MODEL)), xspec, const((1, D_MODEL)), const((1, D_MODEL))],
        out_specs=xspec,
        out_shape=jax.ShapeDtypeStruct((Ts * Bd, D_MODEL), F32),
        compiler_params=_cp(("parallel",)),
        name="ffn_tail_sample",
    )(full, full, full, full, full, full, ffn_conv, ffn_conv, wd_bf, h, ln_g, ln_b)


REQS_PER_STEP = 2


def _sattn_body(pt_ref, q_ref, kn_ref, vn_ref, *rest, n_pages, pos0, ts):
    del pt_ref
    o_ref = rest[2 * n_pages * REQS_PER_STEP]
    for r in range(REQS_PER_STEP):
        pages = rest[2 * n_pages * r:2 * n_pages * (r + 1)]
        o_ref[r] = _sattn_request(q_ref[r], kn_ref[r], vn_ref[r], pages[:n_pages], pages[n_pages:], pos0, ts)


def _sattn_request(qr, kn_new, vn_new, kp, vp, pos0, ts):
    n_pages = len(kp)
    nrow = qr.shape[0]
    ppb = MOBA_BLOCK // PAGE_SIZE
    n_past = n_pages // ppb
    blk_lane = lax.broadcasted_iota(jnp.int32, (BRANCH_W, LANES), 1)
    kmean_t = jnp.zeros((BRANCH_W, LANES), F32)
    for n in range(n_past):
        acc = kp[n * ppb][...]
        for r in range(1, ppb):
            acc = acc + kp[n * ppb + r][...]
        kmean_t = jnp.where(blk_lane == n, jnp.sum(acc, axis=1, keepdims=True) * (1.0 / MOBA_BLOCK), kmean_t)
    gate = jnp.dot(qr, kmean_t, precision=lax.Precision.HIGHEST, preferred_element_type=F32)
    bias = _top_blocks_bias(gate, n_past)
    qb = (qr * (A_DH ** -0.5)).astype(BF)
    scores = []
    for p in range(n_pages):
        s = jnp.dot(qb, kp[p][...].astype(BF), preferred_element_type=F32)
        scores.append(s + bias[:, p // ppb:p // ppb + 1])
    pad = jnp.zeros((LANES - kn_new.shape[0], BRANCH_W), F32)
    kn = jnp.concatenate([kn_new, pad], axis=0).astype(BF)
    vn = jnp.concatenate([vn_new, pad], axis=0).astype(BF)
    s_own = lax.dot_general(qb, kn, NT, preferred_element_type=F32)
    qi = lax.broadcasted_iota(jnp.int32, (nrow, LANES), 0) // A_HEADS
    kj = lax.broadcasted_iota(jnp.int32, (nrow, LANES), 1)
    own_blk = pos0 // MOBA_BLOCK
    ok = (kj < ts) & (own_blk * MOBA_BLOCK + kj <= pos0 + qi)
    s_own = jnp.where(ok, s_own, NEG_INF)
    m = jnp.max(s_own, axis=1, keepdims=True)
    for s in scores:
        m = jnp.maximum(m, jnp.max(s, axis=1, keepdims=True))
    p_own = jnp.exp(s_own - m)
    l = jnp.sum(p_own, axis=1, keepdims=True)
    out = jnp.dot(p_own.astype(BF), vn, preferred_element_type=F32)
    for p in range(n_pages):
        pr = jnp.exp(scores[p] - m)
        l = l + jnp.sum(pr, axis=1, keepdims=True)
        out = out + lax.dot_general(pr.astype(BF), vp[p][...].astype(BF), NT, preferred_element_type=F32)
    out = out / l
    hrow = lax.broadcasted_iota(jnp.int32, (nrow, BRANCH_W), 0) % A_HEADS
    hlane = lax.broadcasted_iota(jnp.int32, (nrow, BRANCH_W), 1) // A_DH
    out = jnp.where(hrow == hlane, out, 0.0)
    return jnp.sum(out.reshape(ts, A_HEADS, BRANCH_W), axis=1)


def _sample_attention(qrows, kn, vn, cache_k, cache_v, page_table, layer, pos0, ts):
    Bd, n_pages = page_table.shape
    nrow = qrows.shape[1]
    rps = REQS_PER_STEP
    assert Bd % rps == 0
    page = lambda r, p: pl.BlockSpec((None, None, BRANCH_W, PAGE_SIZE),
                                     lambda g, pt, r=r, p=p: (layer, pt[(g * rps + r) * n_pages + p], 0, 0))
    req = lambda rows: pl.BlockSpec((rps, rows, BRANCH_W), lambda g, pt: (g, 0, 0))
    pages = [page(r, p) for r in range(rps) for _ in range(2) for p in range(n_pages)]
    caches = [c for _ in range(rps) for c in ([cache_k] * n_pages + [cache_v] * n_pages)]
    grid_spec = pltpu.PrefetchScalarGridSpec(
        num_scalar_prefetch=1,
        grid=(Bd // rps,),
        in_specs=[req(nrow), req(kn.shape[1]), req(vn.shape[1])] + pages,
        out_specs=req(ts),
    )
    return pl.pallas_call(
        functools.partial(_sattn_body, n_pages=n_pages, pos0=pos0, ts=ts),
        grid_spec=grid_spec,
        out_shape=jax.ShapeDtypeStruct((Bd, ts, BRANCH_W), F32),
        compiler_params=_cp(("parallel",)),
        name="moba_sample",
    )(page_table.reshape(-1), qrows, kn, vn, *caches)


def _smix_body(uf_ref, db_ref, dc_ref, dh_ref, zb_ref, pw_ref, ps_ref, sw_ref, yb_ref, yd_ref, zn_ref, *, bd, ts, pos0):
    ps, sw = ps_ref[...], sw_ref[...]
    slab = lambda ref, r: ref[r * bd:(r + 1) * bd, :]
    z = [slab(zb_ref, r) for r in range(SCONV_W - 1)] + [slab(dc_ref, t) * slab(dh_ref, t) for t in range(ts)]
    for t in range(ts):
        u = slab(uf_ref, POOL_BUF + t)
        for g, w in enumerate(POOL_WINDOWS):
            sl = slice(g * POOL_GC, (g + 1) * POOL_GC)
            win = u[:, sl]
            for r in range(1, w):
                win = win + slab(uf_ref, POOL_BUF + t - r)[:, sl]
            pooled = win / float(min(w, pos0 + t + 1)) - u[:, sl]
            y = jnp.dot(pooled.astype(BF), pw_ref[g], preferred_element_type=F32) * ps[:, sl]
            yb_ref[t * bd:(t + 1) * bd, sl] = y.astype(yb_ref.dtype)
        conv = sw[0:1] * z[t] + sw[1:2] * z[t + 1] + sw[2:3] * z[t + 2]
        yd_ref[t * bd:(t + 1) * bd, :] = (slab(db_ref, t) * conv).astype(yd_ref.dtype)
    for r in range(SCONV_W - 1):
        zn_ref[r * bd:(r + 1) * bd, :] = z[ts + r]


def _sample_mixers(ufull, proj, zbuf, pool_w_bf, pool_scale, sconv_w, Bd, Ts, pos0):
    M = Ts * Bd
    whole = lambda a: pl.BlockSpec(a.shape, lambda i, n=a.ndim: (0,) * n)
    blk = lambda cb: pl.BlockSpec((M, BRANCH_W), lambda i, cb=cb: (0, cb))
    out = lambda r, dt: (pl.BlockSpec((r, BRANCH_W), lambda i: (0, 0)), jax.ShapeDtypeStruct((r, BRANCH_W), dt))
    outs = [out(M, BF), out(M, BF), out((SCONV_W - 1) * Bd, F32)]
    return pl.pallas_call(
        functools.partial(_smix_body, bd=Bd, ts=Ts, pos0=pos0),
        grid=(1,),
        in_specs=[whole(ufull), blk(CB_DB), blk(CB_DC), blk(CB_DH), whole(zbuf), whole(pool_w_bf), whole(pool_scale),
                  whole(sconv_w)],
        out_specs=[o[0] for o in outs],
        out_shape=[o[1] for o in outs],
        compiler_params=_cp(("arbitrary",)),
        name="pool_sconv_sample",
    )(ufull, proj, proj, proj, zbuf, pool_w_bf, pool_scale, sconv_w)


def _shgrn_body(cq_ref, cf_ref, ci_ref, cg_ref, s0_ref, lb_ref, ng_ref, yc_ref, so_ref, *, ts):
    lb, ng = lb_ref[...], ng_ref[...]
    for r in range(cq_ref.shape[0]):
        for h in range(C_HEADS):
            sl = slice(h * C_DK, (h + 1) * C_DK)
            o, st = _gla_chunk(cq_ref[r, :, sl], cf_ref[r, :, sl], ci_ref[r, :, sl], lb[:, sl], s0_ref[r, h].T, ts)
            so_ref[r, h] = st.T
            yc_ref[r, :, sl] = _hgrn_out(o, cg_ref[r, :, sl], ng[:, sl])


def _sample_hgrn(cq, cf, ci, cg, s0, lb, ng, ts):
    Bd, rows, _ = cq.shape
    rps = math.gcd(Bd, 4)
    req = pl.BlockSpec((rps, rows, BRANCH_W), lambda b: (b, 0, 0))
    st = pl.BlockSpec((rps, C_HEADS, C_DK, C_DK), lambda b: (b, 0, 0, 0))
    vec = pl.BlockSpec((1, BRANCH_W), lambda b: (0, 0))
    return pl.pallas_call(
        functools.partial(_shgrn_body, ts=ts),
        grid=(Bd // rps,),
        in_specs=[req, req, req, req, st, vec, vec],
        out_specs=[req, st],
        out_shape=[jax.ShapeDtypeStruct((Bd, rows, BRANCH_W), F32), jax.ShapeDtypeStruct(s0.shape, F32)],
        compiler_params=_cp(("parallel",)),
        name="hgrn_sample",
    )(cq, cf, ci, cg, s0, lb, ng)


def _prompt_layer(x, B, T, w, lb, tables):
    tm = min(T, 1024)
    proj = _matmul(x, w["w_in"], tm, N_IN // 4, "proj_prompt")
    kt, vtf, kb, qt, vt, bias = _attn_prep(proj, tables, B, T)
    ya = _prompt_attention(qt, kb, vt, bias, B, T)
    yb, yd, ztail = _prompt_mixers(proj, w["pool_w"], w["pool_scale"], w["sconv_w"], B, T, min(T, 512))
    yc, s_new = _prompt_hgrn(proj, lb, w["hgrn_norm_g"], B, T, min(T, 512))
    h = _merge((ya, yb, yc, yd), proj, x, w["w_branch"], w["w_o"], w["ln1_g"], w["ln1_b"], 256)
    act, up_tail = _prompt_ffn_up(h, w["ffn_up"], w["ffn_conv"], B, T, 256)
    out = _prompt_ffn_down(act, w["ffn_down"], h, w["ln2_g"], w["ln2_b"], 512)
    proj3 = proj.reshape(B, T, N_IN)
    k, v = kt.reshape(B, A_HEADS, A_DH, T), vtf.reshape(B, A_HEADS, A_DH, T)
    new_pool = proj3[:, T - POOL_BUF:, CB_PU * BRANCH_W:(CB_PU + 1) * BRANCH_W]
    new_sconv = ztail[:, 8 - (SCONV_W - 1):]
    new_ffn = up_tail[:, 8 - (FFN_CONV_W - 1):]
    return out, k, v, new_pool, s_new, new_sconv, new_ffn


def _to_tm(a):
    return a.transpose(1, 0, 2).reshape(a.shape[1] * a.shape[0], a.shape[2])


def _from_tm(a, Bd):
    return a.reshape(a.shape[0] // Bd, Bd, a.shape[1]).transpose(1, 0, 2)


def _sample_layer(x, Bd, Ts, pos0, layer, w, lb, tables, cache_k, cache_v, page_table, pool_buf, hgrn_s, sconv_buf, ffn_buf):
    M = Ts * Bd
    proj = _matmul(x, w["w_in"], M, BRANCH_W, "proj_sample")
    qf, _, kf, _, _ = _rope(proj, tables, M)
    col = lambda cb: proj[:, cb * BRANCH_W:(cb + 1) * BRANCH_W]
    q_r, k_r, v_r = _from_tm(qf, Bd), _from_tm(kf, Bd), _from_tm(col(CB_AV), Bd)
    head_of_lane = jnp.arange(BRANCH_W) // A_DH
    hmask = (head_of_lane[None, :] == jnp.arange(A_HEADS)[:, None]).astype(F32)
    qrows = (q_r[:, :, None, :] * hmask[None, None]).reshape(Bd, Ts * A_HEADS, BRANCH_W)
    pad8 = lambda a: jnp.pad(a, ((0, 0), (0, 8 - Ts), (0, 0)))
    ya = _sample_attention(qrows, pad8(k_r), pad8(v_r), cache_k, cache_v, page_table, layer, pos0, Ts)
    ufull = jnp.concatenate([_to_tm(pool_buf), col(CB_PU)], axis=0)
    yb, yd, znew = _sample_mixers(ufull, proj, _to_tm(sconv_buf), w["pool_w"], w["pool_scale"], w["sconv_w"], Bd, Ts, pos0)
    req8 = lambda cb: pad8(_from_tm(col(cb), Bd))
    yc, s_new = _sample_hgrn(req8(CB_CQ), req8(CB_CF), req8(CB_CI), req8(CB_CG), hgrn_s, lb, w["hgrn_norm_g"], Ts)
    ys = (_to_tm(ya), yb, _to_tm(yc[:, :Ts]), yd)
    h = _merge(ys, proj, x, w["w_branch"], w["w_o"], w["ln1_g"], w["ln1_b"], min(M, 256))
    up = _matmul(h, w["ffn_up"], M, BRANCH_W, "ffn_up_sample")
    full = jnp.concatenate([_to_tm(ffn_buf), up], axis=0)
    out = _sample_ffn(full, w["ffn_conv"], w["ffn_down"], h, w["ln2_g"], w["ln2_b"], Bd, Ts)
    k = k_r.reshape(Bd, Ts, A_HEADS, A_DH)
    v = v_r.reshape(Bd, Ts, A_HEADS, A_DH)
    new_pool = _from_tm(ufull[Ts * Bd:], Bd)
    new_sconv = _from_tm(znew, Bd)
    new_ffn = _from_tm(full[Ts * Bd:], Bd)
    return out, k, v, new_pool, s_new, new_sconv, new_ffn


def _layer_weights(l, w_in, w_branch, w_o, pool_w, pool_scale, hgrn_norm_g, sconv_w, ln1_g, ln1_b, ffn_up, ffn_conv,
                   ffn_down, ln2_g, ln2_b):
    row = lambda a: a[l][None, :]
    return {
        "w_in": w_in[l].astype(BF), "w_branch": w_branch[l].astype(BF), "w_o": w_o[l].astype(BF),
        "pool_w": pool_w[l].astype(BF), "pool_scale": row(pool_scale), "hgrn_norm_g": row(hgrn_norm_g),
        "sconv_w": sconv_w[l], "ln1_g": row(ln1_g), "ln1_b": row(ln1_b), "ffn_up": ffn_up[l].astype(BF),
        "ffn_conv": ffn_conv[l], "ffn_down": ffn_down[l].astype(BF), "ln2_g": row(ln2_g), "ln2_b": row(ln2_b),
    }


def kernel(x_prompt, x_sample, cache_k, cache_v, state_pool, state_hgrn, state_sconv, state_ffn, page_table, w_in,
           w_branch, w_o, pool_w, pool_scale, hgrn_lb_logits, hgrn_norm_g, sconv_w, ln1_g, ln1_b, ffn_up, ffn_conv,
           ffn_down, ln2_g, ln2_b):
    depth = w_in.shape[0]
    Bp, T, _ = x_prompt.shape
    Bd, Ts, _ = x_sample.shape
    n_pages = page_table.shape[1]
    pos0 = n_pages * PAGE_SIZE
    assert T % MOBA_BLOCK == 0 and pos0 % MOBA_BLOCK == 0 and Ts <= 8 and T >= HALO
    lb_all = _lower_bounds(hgrn_lb_logits.astype(F32))
    tab_p = _rope_tables(jnp.arange(T))
    tab_s = _rope_tables(pos0 + jnp.repeat(jnp.arange(Ts), Bd))
    ck = cache_k.transpose(0, 1, 3, 4, 2).reshape(depth, cache_k.shape[1], BRANCH_W, PAGE_SIZE)
    cv = cache_v.transpose(0, 1, 3, 4, 2).reshape(depth, cache_v.shape[1], BRANCH_W, PAGE_SIZE)
    hp = x_prompt.reshape(Bp * T, D_MODEL)
    hs = _to_tm(x_sample)
    outs_p, outs_s = [], []
    for l in range(depth):
        w = _layer_weights(l, w_in, w_branch, w_o, pool_w, pool_scale, hgrn_norm_g, sconv_w, ln1_g, ln1_b, ffn_up,
                           ffn_conv, ffn_down, ln2_g, ln2_b)
        lb = lb_all[l][None, :]
        rp = _prompt_layer(hp, Bp, T, w, lb, tab_p)
        rs = _sample_layer(hs, Bd, Ts, pos0, l, w, lb, tab_s, ck, cv, page_table, state_pool[l], state_hgrn[l],
                           state_sconv[l], state_ffn[l])
        hp, hs = rp[0], rs[0]
        outs_p.append(rp[1:])
        outs_s.append(rs[1:])
    stack = lambda outs: tuple(jnp.stack([o[j] for o in outs]) for j in range(6))
    sp = stack(outs_p)
    tokens_first = lambda a: a.transpose(0, 1, 4, 2, 3)
    sp = (tokens_first(sp[0]), tokens_first(sp[1])) + sp[2:]
    return (hp.reshape(Bp, T, D_MODEL), _from_tm(hs, Bd)) + sp + stack(outs_s)
```

```python
import functools
import math

import jax
import jax.numpy as jnp
import numpy as np
from jax import lax
from jax.experimental import pallas as pl
from jax.experimental.pallas import tpu as pltpu

F32 = jnp.float32
BF = jnp.bfloat16
NEG_INF = float("-inf")
LOG2_E = math.log2(math.e)

D_MODEL = 1024
BRANCH_W = 512
N_BRANCH = 4
A_HEADS = 8
A_DH = 64
MOBA_BLOCK = 256
MOBA_TOPK = 3
ROPE_DIM = A_DH // 4
ROPE_THETA = 500000.0
POOL_WINDOWS = (2, 4, 8, 16)
POOL_GC = 128
POOL_BUF = 15
C_HEADS = 4
C_DK = 128
SCONV_W = 3
D_FF = 2816
FFN_CONV_W = 3
PAGE_SIZE = 128
DEPTH = 4
DN_ALPHA = (2 * DEPTH) ** 0.25
LN_EPS = 1e-5
RMS_EPS = 1e-6

N_IN = (11 + 2 * N_BRANCH) * BRANCH_W
CB_GATE = 0
CB_CQ, CB_CF, CB_CI, CB_CG = 8, 9, 10, 11
CB_AQ, CB_AK, CB_AV = 12, 13, 14
CB_DB, CB_DC, CB_DH = 15, 16, 17
CB_PU = 18
COLUMN_BLOCK_ORDER = tuple(range(11, 19)) + (4, 5, 6, 7) + (0, 1, 2) + (8, 9, 10) + (3,)

LANES = 128
HALO = 16
SUM_ROWS = 16
VMEM_LIMIT = 56 * 1024 * 1024

NT = (((1,), (1,)), ((), ()))
TN = (((0,), (0,)), ((), ()))


def _cp(sem, vmem=VMEM_LIMIT):
    return pltpu.CompilerParams(dimension_semantics=sem, vmem_limit_bytes=vmem)


def _sigmoid(x):
    return 1.0 / (1.0 + jnp.exp(-x))


def _silu(x):
    return x * _sigmoid(x)


def _layer_norm(x, g, b):
    mu = jnp.mean(x, axis=-1, keepdims=True)
    xc = x - mu
    var = jnp.mean(xc * xc, axis=-1, keepdims=True)
    return xc * lax.rsqrt(var + LN_EPS) * g + b


def _lb_body(logit_ref, o_ref):
    x = logit_ref[...]
    m = jnp.max(x, axis=0, keepdims=True)
    e = jnp.exp(x - m)
    p = e / jnp.sum(e, axis=0, keepdims=True)
    acc = jnp.zeros_like(p[0:1])
    rows = []
    for l in range(x.shape[0]):
        acc = acc + p[l:l + 1]
        rows.append(acc - p[0:1])
    o_ref[...] = jnp.concatenate(rows, axis=0)


def _lower_bounds(logits):
    return pl.pallas_call(_lb_body, out_shape=jax.ShapeDtypeStruct(logits.shape, F32), name="hgrn_lb")(logits)


def _mm_body(x_ref, w_ref, o_ref, xb_ref):
    @pl.when(pl.program_id(1) == 0)
    def _():
        xb_ref[...] = x_ref[...].astype(BF)

    o_ref[...] = jnp.dot(xb_ref[...], w_ref[...], preferred_element_type=F32)


def _matmul(x, w_bf, tm, tn, name):
    M, K = x.shape
    N = w_bf.shape[1]
    assert M % tm == 0 and N % tn == 0
    return pl.pallas_call(
        _mm_body,
        grid=(M // tm, N // tn),
        in_specs=[pl.BlockSpec((tm, K), lambda i, j: (i, 0)),
                  pl.BlockSpec((K, tn), lambda i, j: (0, j))],
        out_specs=pl.BlockSpec((tm, tn), lambda i, j: (i, j)),
        out_shape=jax.ShapeDtypeStruct((M, N), F32),
        scratch_shapes=[pltpu.VMEM((tm, K), BF)],
        compiler_params=_cp(("parallel", "arbitrary")),
        name=name,
    )(x, w_bf)


def _rope_tables(pos):
    half = ROPE_DIM // 2
    inv = jnp.power(ROPE_THETA, -jnp.arange(0, ROPE_DIM, 2, dtype=F32) / ROPE_DIM)
    ang = pos.astype(F32)[:, None] * inv[None, :]
    cos, sin = jnp.cos(ang), jnp.sin(ang)
    n = pos.shape[0]
    ones = jnp.ones((n, A_DH - ROPE_DIM), F32)
    zeros = jnp.zeros((n, A_DH - ROPE_DIM), F32)
    zh = jnp.zeros((n, half), F32)
    c = jnp.concatenate([cos, cos, ones], axis=1)
    s_lo = jnp.concatenate([-sin, zh, zeros], axis=1)
    s_hi = jnp.concatenate([zh, sin, zeros], axis=1)
    tile = lambda a: jnp.tile(a, (1, A_HEADS))
    return tile(c), tile(s_lo), tile(s_hi)


def _rope_body(q_ref, k_ref, v_ref, c_ref, sl_ref, sh_ref, qf_ref, qb_ref, kf_ref, kb_ref, vb_ref):
    c, sl, sh = c_ref[...], sl_ref[...], sh_ref[...]
    half = ROPE_DIM // 2
    w = q_ref.shape[1]

    def rope(x):
        return x * c + pltpu.roll(x, w - half, 1) * sl + pltpu.roll(x, half, 1) * sh

    q = rope(q_ref[...])
    k = rope(k_ref[...])
    qf_ref[...] = q
    qb_ref[...] = (q * (A_DH ** -0.5)).astype(BF)
    kf_ref[...] = k
    kb_ref[...] = k.astype(BF)
    vb_ref[...] = v_ref[...].astype(BF)


def _rope(proj, tables, tm):
    M = proj.shape[0]
    ntab = tables[0].shape[0] // tm
    blk = lambda cb: pl.BlockSpec((tm, BRANCH_W), lambda i, cb=cb: (i, cb))
    tab = pl.BlockSpec((tm, BRANCH_W), lambda i: (i % ntab, 0))
    out = pl.BlockSpec((tm, BRANCH_W), lambda i: (i, 0))
    sds = lambda dt: jax.ShapeDtypeStruct((M, BRANCH_W), dt)
    return pl.pallas_call(
        _rope_body,
        grid=(M // tm,),
        in_specs=[blk(CB_AQ), blk(CB_AK), blk(CB_AV), tab, tab, tab],
        out_specs=[out] * 5,
        out_shape=[sds(F32), sds(BF), sds(F32), sds(BF), sds(BF)],
        compiler_params=_cp(("parallel",)),
        name="rope",
    )(proj, proj, proj, *tables)


def _top_blocks_bias(gate, n_past):
    col = lax.broadcasted_iota(jnp.int32, gate.shape, 1)
    g = jnp.where(col < n_past, gate, NEG_INF)
    bias = jnp.full(gate.shape, NEG_INF, F32)
    for _ in range(MOBA_TOPK):
        m = jnp.max(g, axis=1, keepdims=True)
        idx = jnp.min(jnp.where(g == m, col, LANES), axis=1, keepdims=True)
        pick = (col == idx) & (m > NEG_INF)
        bias = jnp.where(pick, 0.0, bias)
        g = jnp.where(pick, NEG_INF, g)
    return bias


def _split_bf16(x):
    hi = x.astype(BF)
    return hi, (x - hi.astype(F32)).astype(BF)


def _prep_body(qkv_ref, c_ref, sl_ref, sh_ref, kt_ref, vtf_ref, kb_ref, qt_ref, vt_ref, bias_ref, km_ref):
    i = pl.program_id(1)
    nbp = km_ref.shape[0]
    q_ref, k_ref, v_ref = (qkv_ref.at[:, n * BRANCH_W:(n + 1) * BRANCH_W] for n in range(3))

    @pl.when(i == 0)
    def _():
        km_ref[...] = jnp.zeros_like(km_ref)

    c, sl, sh = c_ref[...], sl_ref[...], sh_ref[...]
    half = ROPE_DIM // 2

    def rope(x):
        return x * c + pltpu.roll(x, BRANCH_W - half, 1) * sl + pltpu.roll(x, half, 1) * sh

    q = rope(q_ref[...])
    k = rope(k_ref[...])
    kt_ref[...] = k.T
    kb_ref[...] = k.astype(BF)
    qt_ref[...] = (q * (A_DH ** -0.5 * LOG2_E)).T.astype(BF)
    v_t = v_ref[...].T
    vtf_ref[...] = v_t
    vt_ref[...] = v_t.astype(BF)

    km = km_ref[...]
    lane = lax.broadcasted_iota(jnp.int32, km.shape, 1)
    stack = jnp.concatenate([jnp.where(lane // A_DH == h, km, 0.0) for h in range(A_HEADS)], axis=0)
    k_hi, k_lo = _split_bf16(stack)
    q_hi, q_lo = _split_bf16(q)
    dot = lambda a, b: lax.dot_general(a, b, NT, preferred_element_type=F32)
    gate = (dot(k_hi, q_hi) + dot(k_lo, q_hi) + dot(k_hi, q_lo)).reshape(A_HEADS, nbp, MOBA_BLOCK)
    blk = lax.broadcasted_iota(jnp.int32, gate.shape, 1)
    gate = jnp.where(blk < i, gate, NEG_INF)
    bias = jnp.full(gate.shape, NEG_INF, F32)
    for _ in range(MOBA_TOPK):
        m = jnp.max(gate, axis=1, keepdims=True)
        idx = jnp.min(jnp.where(gate == m, blk, nbp), axis=1, keepdims=True)
        pick = (blk == idx) & (m > NEG_INF)
        bias = jnp.where(pick, 0.0, bias)
        gate = jnp.where(pick, NEG_INF, gate)
    bias_ref[0] = bias.reshape(A_HEADS * nbp, MOBA_BLOCK)
    km_ref[pl.ds(i, 1), :] = jnp.sum(k, axis=0, keepdims=True) * (1.0 / MOBA_BLOCK)


def _attn_prep(proj, tables, B, T):
    nq = T // MOBA_BLOCK
    nbp = -(-nq // 8) * 8
    tq = MOBA_BLOCK
    assert (CB_AQ, CB_AK, CB_AV) == tuple(range(CB_AQ, CB_AQ + 3)) and CB_AQ % 3 == 0
    qkv = pl.BlockSpec((tq, 3 * BRANCH_W), lambda b, i: (b * nq + i, CB_AQ // 3))
    tab = pl.BlockSpec((tq, BRANCH_W), lambda b, i: (i, 0))
    row = pl.BlockSpec((tq, BRANCH_W), lambda b, i: (b * nq + i, 0))
    tr = pl.BlockSpec((BRANCH_W, tq), lambda b, i: (b, i))
    return pl.pallas_call(
        _prep_body,
        grid=(B, nq),
        in_specs=[qkv, tab, tab, tab],
        out_specs=[tr, tr, row, tr, tr, pl.BlockSpec((1, A_HEADS * nbp, tq), lambda b, i: (b, 0, i))],
        out_shape=[jax.ShapeDtypeStruct((B * BRANCH_W, T), F32), jax.ShapeDtypeStruct((B * BRANCH_W, T), F32),
                   jax.ShapeDtypeStruct((B * T, BRANCH_W), BF),
                   jax.ShapeDtypeStruct((B * BRANCH_W, T), BF), jax.ShapeDtypeStruct((B * BRANCH_W, T), BF),
                   jax.ShapeDtypeStruct((B, A_HEADS * nbp, T), F32)],
        scratch_shapes=[pltpu.VMEM((nbp, BRANCH_W), F32)],
        compiler_params=_cp(("parallel", "arbitrary")),
        name="attn_prep",
    )(proj, *tables)


def _attn_body(qt_ref, k_ref, vt_ref, bias_ref, o_ref, sa_ref, sb_ref, pp_ref):
    i = pl.program_id(2)
    tq = MOBA_BLOCK
    npair = LANES // A_DH
    nbp = bias_ref.shape[1] // npair
    qt = qt_ref[...]
    rowi = lax.broadcasted_iota(jnp.int32, qt.shape, 0)
    qts = [jnp.where((rowi >= e * A_DH) & (rowi < (e + 1) * A_DH), qt, jnp.zeros_like(qt)) for e in range(npair)]
    block = lambda j: pl.ds(j * tq if isinstance(j, int) else pl.multiple_of(j * tq, tq), tq)
    ones = jnp.ones((SUM_ROWS, tq), BF)

    def scores(s_ref, j):
        kj = k_ref[block(j), :]
        for e in range(npair):
            s_ref[e] = jnp.dot(kj, qts[e], preferred_element_type=F32)

    def pending_product(e, jp):
        vt = jnp.concatenate([vt_ref[e * A_DH:(e + 1) * A_DH, block(jp)], ones], axis=0)
        return jnp.dot(vt, pp_ref[e], preferred_element_type=F32)

    def settle(e, head, jp):
        m, l, acc, pa = head
        pv = pending_product(e, jp)
        return m, pa * l + pv[A_DH:A_DH + 1], pa * acc + pv[:A_DH]

    def consume(s_ref, j, carry, own):
        heads, jp = carry
        new = []
        for e in range(npair):
            m, l, acc = settle(e, heads[e], jp)
            s = s_ref[e]
            if own:
                causal = (lax.broadcasted_iota(jnp.int32, (tq, tq), 0) <= lax.broadcasted_iota(jnp.int32, (tq, tq), 1))
                s = jnp.where(causal, s, NEG_INF)
                mn = jnp.maximum(m, jnp.max(s, axis=0, keepdims=True))
                shift = mn
            else:
                b = bias_ref[0, pl.ds(e * nbp + j, 1), :]
                mn = jnp.maximum(m, jnp.max(s, axis=0, keepdims=True) + b)
                shift = mn - b
            pp_ref[e] = jnp.exp2(s - shift).astype(BF)
            new.append((mn, l, acc, jnp.exp2(m - mn)))
        return tuple(new), j

    def two_blocks(p, carry):
        scores(sb_ref, 2 * p + 1)
        carry = consume(sa_ref, 2 * p, carry, own=False)
        scores(sa_ref, 2 * p + 2)
        return consume(sb_ref, 2 * p + 1, carry, own=False)

    def odd_tail(carry):
        scores(sb_ref, i)
        carry = consume(sa_ref, i - 1, carry, own=False)
        return consume(sb_ref, i, carry, own=True)

    def even_tail(carry):
        return consume(sa_ref, i, carry, own=True)

    pp_ref[...] = jnp.zeros_like(pp_ref)
    init = tuple((jnp.full((1, tq), -1e30, F32), jnp.zeros((1, tq), F32), jnp.zeros((A_DH, tq), F32),
                  jnp.ones((1, tq), F32)) for _ in range(npair))
    scores(sa_ref, 0)
    state = lax.fori_loop(0, i // 2, two_blocks, (init, jnp.int32(0)))
    heads, jp = lax.cond(i % 2 == 1, odd_tail, even_tail, state)
    outs = []
    for e in range(npair):
        _, l, acc = settle(e, heads[e], jp)
        outs.append(acc / l)
    o_ref[...] = jnp.concatenate(outs, axis=0).T.astype(o_ref.dtype)


def _prompt_attention(qt, kb, vt, bias, B, T):
    nq = T // MOBA_BLOCK
    npair = LANES // A_DH
    ngrp = BRANCH_W // LANES
    nbp = bias.shape[1] // A_HEADS
    return pl.pallas_call(
        _attn_body,
        grid=(B, ngrp, nq),
        in_specs=[pl.BlockSpec((LANES, MOBA_BLOCK), lambda b, hp, i: (b * ngrp + hp, i)),
                  pl.BlockSpec((T, LANES), lambda b, hp, i: (b, hp)),
                  pl.BlockSpec((LANES, T), lambda b, hp, i: (b * ngrp + hp, 0)),
                  pl.BlockSpec((1, npair * nbp, MOBA_BLOCK), lambda b, hp, i: (b, hp, i))],
        out_specs=pl.BlockSpec((MOBA_BLOCK, LANES), lambda b, hp, i: (b * nq + i, hp)),
        out_shape=jax.ShapeDtypeStruct((B * T, BRANCH_W), BF),
        scratch_shapes=[pltpu.VMEM((npair, MOBA_BLOCK, MOBA_BLOCK), F32)] * 2
        + [pltpu.VMEM((npair, MOBA_BLOCK, MOBA_BLOCK), BF)],
        compiler_params=_cp(("parallel", "parallel", "arbitrary")),
        name="moba_prompt",
    )(qt, kb, vt, bias)


def _pool_group(ext_ref, u, g, w, t_first, pw, ps):
    tm = u.shape[0]
    sl = slice(g * POOL_GC, (g + 1) * POOL_GC)
    win = ext_ref[HALO:HALO + tm, sl]
    for r in range(1, w):
        win = win + ext_ref[HALO - r:HALO - r + tm, sl]
    t = t_first + lax.broadcasted_iota(jnp.int32, (tm, POOL_GC), 0)
    cnt = jnp.minimum(w, t + 1).astype(F32)
    pooled = win / cnt - u[:, sl]
    return jnp.dot(pooled.astype(BF), pw, preferred_element_type=F32) * ps[:, sl]


def _mix_body(u_ref, uh_ref, db_ref, dc_ref, dh_ref, dch_ref, dhh_ref, pw_ref, ps_ref, sw_ref,
              yb_ref, yd_ref, zt_ref, ext_ref, zext_ref):
    i = pl.program_id(1)
    tm = u_ref.shape[0]
    first = i == 0
    u = u_ref[...]
    ext_ref[0:HALO, :] = jnp.where(first, 0.0, uh_ref[...])
    ext_ref[HALO:, :] = u
    ps = ps_ref[...]
    for g, w in enumerate(POOL_WINDOWS):
        y = _pool_group(ext_ref, u, g, w, i * tm, pw_ref[g], ps)
        yb_ref[:, g * POOL_GC:(g + 1) * POOL_GC] = y.astype(yb_ref.dtype)
    z = dc_ref[...] * dh_ref[...]
    zext_ref[0:8, :] = jnp.where(first, 0.0, dch_ref[...] * dhh_ref[...])
    zext_ref[8:, :] = z
    sw = sw_ref[...]
    conv = sw[0:1] * zext_ref[6:6 + tm, :] + sw[1:2] * zext_ref[7:7 + tm, :] + sw[2:3] * z
    yd_ref[...] = (db_ref[...] * conv).astype(yd_ref.dtype)
    zt_ref[0] = z[tm - 8:tm]


def _prompt_mixers(proj, pool_w_bf, pool_scale, sconv_w, B, T, tm):
    nt = T // tm
    row = lambda b, i: b * nt + i
    blk = lambda cb: pl.BlockSpec((tm, BRANCH_W), lambda b, i, cb=cb: (row(b, i), cb))

    def halo(cb, h):
        return pl.BlockSpec((h, BRANCH_W), lambda b, i, cb=cb, h=h: (jnp.maximum(row(b, i) * (tm // h) - 1, 0), cb))

    const = lambda shape: pl.BlockSpec(shape, lambda b, i, n=len(shape): (0,) * n)
    return pl.pallas_call(
        _mix_body,
        grid=(B, nt),
        in_specs=[blk(CB_PU), halo(CB_PU, HALO), blk(CB_DB), blk(CB_DC), blk(CB_DH), halo(CB_DC, 8), halo(CB_DH, 8),
                  const((len(POOL_WINDOWS), POOL_GC, POOL_GC)), const((1, BRANCH_W)), const((SCONV_W, BRANCH_W))],
        out_specs=[pl.BlockSpec((tm, BRANCH_W), lambda b, i: (row(b, i), 0)),
                   pl.BlockSpec((tm, BRANCH_W), lambda b, i: (row(b, i), 0)),
                   pl.BlockSpec((1, 8, BRANCH_W), lambda b, i: (b, 0, 0))],
        out_shape=[jax.ShapeDtypeStruct((B * T, BRANCH_W), BF), jax.ShapeDtypeStruct((B * T, BRANCH_W), BF),
                   jax.ShapeDtypeStruct((B, 8, BRANCH_W), F32)],
        scratch_shapes=[pltpu.VMEM((tm + HALO, BRANCH_W), F32), pltpu.VMEM((tm + 8, BRANCH_W), F32)],
        compiler_params=_cp(("parallel", "arbitrary")),
        name="pool_sconv_prompt",
    )(proj, proj, proj, proj, proj, proj, proj, pool_w_bf, pool_scale, sconv_w)


def _gla_chunk(cq, cf, ci, lb, st, n_valid):
    rows = cq.shape[0]
    f = lb + (1.0 - lb) * _sigmoid(cf)
    g = jnp.log(f)
    r_i = lax.broadcasted_iota(jnp.int32, (rows, rows), 0)
    c_i = lax.broadcasted_iota(jnp.int32, (rows, rows), 1)
    tri = jnp.where((r_i >= c_i) & (c_i < n_valid), 1.0, 0.0).astype(F32)
    b = jnp.dot(tri, g, precision=lax.Precision.HIGHEST, preferred_element_type=F32)
    qs = _silu(cq)
    kk = 1.0 - f
    t_i = lax.broadcasted_iota(jnp.int32, (rows, 1), 0)
    o = jnp.zeros((rows, C_DK), F32)
    for s in range(n_valid):
        e = jnp.exp(jnp.minimum(b - b[s:s + 1], 0.0))
        a = jnp.sum(qs * kk[s:s + 1] * e, axis=1, keepdims=True)
        o = o + jnp.where(t_i >= s, a, 0.0) * ci[s:s + 1]
    o = o + lax.dot_general((qs * jnp.exp(b)).astype(BF), st.astype(BF), NT, preferred_element_type=F32)
    bl = b[n_valid - 1:n_valid]
    ks = jnp.where(t_i < n_valid, kk * jnp.exp(jnp.minimum(bl - b, 0.0)), 0.0)
    upd = lax.dot_general(ci.astype(BF), ks.astype(BF), TN, preferred_element_type=F32)
    return o, st * jnp.exp(bl) + upd


def _hgrn_out(o, cg, ng):
    o = o * lax.rsqrt(jnp.mean(o * o, axis=1, keepdims=True) + RMS_EPS)
    return o * ng * _silu(cg)


GROUP = 128
GROUP_HALVES = tuple(GROUP >> (s + 1) for s in range(int(math.log2(GROUP))))


def _decay_sum_matrix():
    t = np.arange(GROUP)[:, None]
    u = np.arange(GROUP)[None, :]
    blocks = [u <= t, u > t]
    for h in GROUP_HALVES:
        r = (t // (2 * h)) * 2 * h + h - 1
        second = (t // h) % 2 == 1
        blocks.append(np.where(second, (u > r) & (u <= t), (u > t) & (u <= r)))
    d = np.concatenate(blocks, axis=0).astype(np.float32)
    return jnp.asarray(np.concatenate([d, d], axis=1), dtype=BF)


def _gla_decays(cf, lb, dmat):
    f = lb + (1.0 - lb) * _sigmoid(cf)
    g_hi, g_lo = _split_bf16(jnp.log(f))
    return f, jnp.dot(dmat, jnp.concatenate([g_hi, g_lo], axis=0), preferred_element_type=F32)


def _gla_group(cq, f, x, ci, st):
    e = jnp.exp(x)
    blk = lambda n: e[n * GROUP:(n + 1) * GROUP]
    qs = _silu(cq)
    kk = 1.0 - f
    v = ci.astype(BF)
    t_i = lax.broadcasted_iota(jnp.int32, (GROUP, GROUP), 0)
    s_i = lax.broadcasted_iota(jnp.int32, (GROUP, GROUP), 1)
    row = lax.broadcasted_iota(jnp.int32, (GROUP, C_DK), 0)
    dot_nt = lambda a, b: lax.dot_general(a.astype(BF), b.astype(BF), NT, preferred_element_type=F32)
    attn = jnp.where(t_i == s_i, dot_nt(qs, kk), 0.0)
    for s, h in enumerate(GROUP_HALVES):
        second = (row // h) % 2 == 1
        scaled_q = jnp.where(second, qs * blk(2 + s), 0.0)
        scaled_k = jnp.where(second, 0.0, kk * blk(2 + s))
        part = dot_nt(scaled_q, scaled_k)
        attn = attn + (part if 2 * h == GROUP else jnp.where(t_i // (2 * h) == s_i // (2 * h), part, 0.0))
    o = jnp.dot(attn.astype(BF), v, preferred_element_type=F32) + dot_nt(qs * blk(0), st)
    upd = lax.dot_general(v, (kk * blk(1)).astype(BF), TN, preferred_element_type=F32)
    return o, st * e[GROUP - 1:GROUP] + upd


def _hgrn_body(c_ref, lb_ref, ng_ref, dm_ref, yc_ref, so_ref, st_ref):
    it = pl.program_id(1)
    tc = c_ref.shape[0]
    cq_ref, cf_ref, ci_ref, cg_ref = (c_ref.at[:, n * BRANCH_W:(n + 1) * BRANCH_W] for n in range(4))

    @pl.when(it == 0)
    def _():
        st_ref[...] = jnp.zeros_like(st_ref)

    lb, ng = lb_ref[...], ng_ref[...]

    def group(gi, carry):
        rows = pl.ds(pl.multiple_of(gi * GROUP, GROUP), GROUP)
        head = lambda h: slice(h * C_DK, (h + 1) * C_DK)
        f, x = _gla_decays(cf_ref[rows, :], lb, dm_ref[...])
        for h in range(C_HEADS):
            sl = head(h)
            o, st = _gla_group(cq_ref[rows, sl], f[:, sl], x[:, sl], ci_ref[rows, sl], st_ref[h])
            st_ref[h] = st
            yc_ref[rows, sl] = _hgrn_out(o, cg_ref[rows, sl], ng[:, sl]).astype(yc_ref.dtype)
        return carry

    lax.fori_loop(0, tc // GROUP, group, 0)

    @pl.when(it == pl.num_programs(1) - 1)
    def _():
        for h in range(C_HEADS):
            so_ref[0, h] = st_ref[h].T


def _prompt_hgrn(proj, lb, ng, B, T, tc):
    nt = T // tc
    dmat = _decay_sum_matrix()
    assert (CB_CQ, CB_CF, CB_CI, CB_CG) == tuple(range(CB_CQ, CB_CQ + 4)) and CB_CQ % 4 == 0
    vec = pl.BlockSpec((1, BRANCH_W), lambda b, i: (0, 0))
    return pl.pallas_call(
        _hgrn_body,
        grid=(B, nt),
        in_specs=[pl.BlockSpec((tc, 4 * BRANCH_W), lambda b, i: (b * nt + i, CB_CQ // 4)), vec, vec,
                  pl.BlockSpec(dmat.shape, lambda b, i: (0, 0))],
        out_specs=[pl.BlockSpec((tc, BRANCH_W), lambda b, i: (b * nt + i, 0)),
                   pl.BlockSpec((1, C_HEADS, C_DK, C_DK), lambda b, i: (b, 0, 0, 0))],
        out_shape=[jax.ShapeDtypeStruct((B * T, BRANCH_W), BF), jax.ShapeDtypeStruct((B, C_HEADS, C_DK, C_DK), F32)],
        scratch_shapes=[pltpu.VMEM((C_HEADS, C_DK, C_DK), F32)],
        compiler_params=_cp(("parallel", "arbitrary")),
        name="hgrn_prompt",
    )(proj, lb, ng, dmat)


def _merge_body(ya_ref, yb_ref, yc_ref, yd_ref, g_ref, x_ref, wb_ref, wo_ref, lg_ref, lbias_ref, h_ref):
    mixed = jnp.zeros(x_ref.shape, F32)
    for n, y_ref in enumerate((ya_ref, yb_ref, yc_ref, yd_ref)):
        br = jnp.dot(y_ref[...].astype(BF), wb_ref[n], preferred_element_type=F32)
        mixed = mixed + _sigmoid(g_ref[:, n * D_MODEL:(n + 1) * D_MODEL]) * br
    pre = DN_ALPHA * x_ref[...] + jnp.dot(mixed.astype(BF), wo_ref[...], preferred_element_type=F32)
    h_ref[...] = _layer_norm(pre, lg_ref[...], lbias_ref[...])


def _merge(ys, proj, x, wb_bf, wo_bf, ln_g, ln_b, tm):
    M = x.shape[0]
    gate_w = N_BRANCH * D_MODEL
    assert (CB_GATE * BRANCH_W) % gate_w == 0
    yspec = pl.BlockSpec((tm, BRANCH_W), lambda i: (i, 0))
    gspec = pl.BlockSpec((tm, gate_w), lambda i: (i, CB_GATE * BRANCH_W // gate_w))
    xspec = pl.BlockSpec((tm, D_MODEL), lambda i: (i, 0))
    const = lambda shape: pl.BlockSpec(shape, lambda i, n=len(shape): (0,) * n)
    return pl.pallas_call(
        _merge_body,
        grid=(M // tm,),
        in_specs=[yspec] * N_BRANCH + [gspec, xspec, const((N_BRANCH, BRANCH_W, D_MODEL)), const((D_MODEL, D_MODEL)),
                                       const((1, D_MODEL)), const((1, D_MODEL))],
        out_specs=xspec,
        out_shape=jax.ShapeDtypeStruct((M, D_MODEL), F32),
        compiler_params=_cp(("parallel",)),
        name="merge_ln1",
    )(*ys, proj, x, wb_bf, wo_bf, ln_g, ln_b)


def _ffn_finish(gate, val, wd_ref, h_ref, lg_ref, lb_ref, o_ref):
    a = (_silu(gate) * val).astype(BF)
    pre = DN_ALPHA * h_ref[...] + jnp.dot(a, wd_ref[...], preferred_element_type=F32)
    o_ref[...] = _layer_norm(pre, lg_ref[...], lb_ref[...])


FFN_COLS = 256


def _ffn_up_body(x_ref, w_ref, cw_ref, a_ref, tail_ref, halo_ref, *, tiles_per_seq):
    i = pl.program_id(0)
    tm = x_ref.shape[0]

    @pl.when(i == 0)
    def _():
        halo_ref[...] = jnp.zeros_like(halo_ref)

    xb = x_ref[...].astype(BF)
    seq_start = i % tiles_per_seq == 0

    def conv(cols):
        u = jnp.dot(xb, w_ref[:, cols], preferred_element_type=F32)
        prev = jnp.where(seq_start, 0.0, halo_ref[:, cols])
        halo_ref[:, cols] = u[tm - 8:tm]
        tail_ref[0, :, cols] = u[tm - 8:tm]
        cw = cw_ref[:, cols]
        back1 = jnp.concatenate([prev[7:8], u[:tm - 1]], axis=0)
        back2 = jnp.concatenate([prev[6:8], u[:tm - 2]], axis=0)
        return cw[0:1] * back2 + cw[1:2] * back1 + cw[2:3] * u

    for c in range(D_FF // FFN_COLS):
        gate = conv(slice(c * FFN_COLS, (c + 1) * FFN_COLS))
        val = conv(slice(D_FF + c * FFN_COLS, D_FF + (c + 1) * FFN_COLS))
        a_ref[:, c * FFN_COLS:(c + 1) * FFN_COLS] = (_silu(gate) * val).astype(a_ref.dtype)


def _prompt_ffn_up(h, wu_bf, ffn_conv, B, T, tm):
    M, K = h.shape
    nt = T // tm
    assert D_FF % FFN_COLS == 0
    const = lambda shape: pl.BlockSpec(shape, lambda i, n=len(shape): (0,) * n)
    return pl.pallas_call(
        functools.partial(_ffn_up_body, tiles_per_seq=nt),
        grid=(M // tm,),
        in_specs=[pl.BlockSpec((tm, K), lambda i: (i, 0)), const((K, 2 * D_FF)), const((FFN_CONV_W, 2 * D_FF))],
        out_specs=[pl.BlockSpec((tm, D_FF), lambda i: (i, 0)),
                   pl.BlockSpec((1, 8, 2 * D_FF), lambda i: (i // nt, 0, 0))],
        out_shape=[jax.ShapeDtypeStruct((M, D_FF), BF), jax.ShapeDtypeStruct((B, 8, 2 * D_FF), F32)],
        scratch_shapes=[pltpu.VMEM((8, 2 * D_FF), F32)],
        compiler_params=_cp(("arbitrary",)),
        name="ffn_up_act_prompt",
    )(h, wu_bf, ffn_conv)


def _ffn_down_body(a_ref, wd_ref, h_ref, lg_ref, lb_ref, o_ref):
    pre = DN_ALPHA * h_ref[...] + jnp.dot(a_ref[...], wd_ref[...], preferred_element_type=F32)
    o_ref[...] = _layer_norm(pre, lg_ref[...], lb_ref[...])


def _prompt_ffn_down(a, wd_bf, h, ln_g, ln_b, tm):
    M = h.shape[0]
    const = lambda shape: pl.BlockSpec(shape, lambda i, n=len(shape): (0,) * n)
    xspec = pl.BlockSpec((tm, D_MODEL), lambda i: (i, 0))
    return pl.pallas_call(
        _ffn_down_body,
        grid=(M // tm,),
        in_specs=[pl.BlockSpec((tm, D_FF), lambda i: (i, 0)), const((D_FF, D_MODEL)), xspec,
                  const((1, D_MODEL)), const((1, D_MODEL))],
        out_specs=xspec,
        out_shape=jax.ShapeDtypeStruct((M, D_MODEL), F32),
        compiler_params=_cp(("parallel",)),
        name="ffn_down_prompt",
    )(a, wd_bf, h, ln_g, ln_b)


def _ffn_sample_body(g0_ref, g1_ref, g2_ref, v0_ref, v1_ref, v2_ref, cwg_ref, cwv_ref, wd_ref, h_ref, lg_ref, lb_ref, o_ref):
    cwg, cwv = cwg_ref[...], cwv_ref[...]
    gate = cwg[0:1] * g0_ref[...] + cwg[1:2] * g1_ref[...] + cwg[2:3] * g2_ref[...]
    val = cwv[0:1] * v0_ref[...] + cwv[1:2] * v1_ref[...] + cwv[2:3] * v2_ref[...]
    _ffn_finish(gate, val, wd_ref, h_ref, lg_ref, lb_ref, o_ref)


def _sample_ffn(full, ffn_conv, wd_bf, h, ln_g, ln_b, Bd, Ts):
    blk = lambda c, r: pl.BlockSpec((Bd, D_FF), lambda t, c=c, r=r: (t + r, c))
    cw = lambda c: pl.BlockSpec((FFN_CONV_W, D_FF), lambda t, c=c: (0, c))
    const = lambda shape: pl.BlockSpec(shape, lambda t, n=len(shape): (0,) * n)
    xspec = pl.BlockSpec((Bd, D_MODEL), lambda t: (t, 0))
    return pl.pallas_call(
        _ffn_sample_body,
        grid=(Ts,),
        in_specs=[blk(0, 0), blk(0, 1), blk(0, 2), blk(1, 0), blk(1, 1), blk(1, 2), cw(0), cw(1),
                  const((D_FF, D_MODEL)), xspec, const((1, D_MODEL)), const((1, D_MODEL))],
        out_specs=xspec,
        out_shape=jax.ShapeDtypeStruct((Ts * Bd, D_MODEL), F32),
        compiler_params=_cp(("parallel",)),
        name="ffn_tail_sample",
    )(full, full, full, full, full, full, ffn_conv, ffn_conv, wd_bf, h, ln_g, ln_b)


REQS_PER_STEP = 2


def _sattn_body(pt_ref, q_ref, kn_ref, vn_ref, *rest, n_pages, pos0, ts):
    del pt_ref
    o_ref = rest[2 * n_pages * REQS_PER_STEP]
    for r in range(REQS_PER_STEP):
        pages = rest[2 * n_pages * r:2 * n_pages * (r + 1)]
        o_ref[r] = _sattn_request(q_ref[r], kn_ref[r], vn_ref[r], pages[:n_pages], pages[n_pages:], pos0, ts)


def _sattn_request(qr, kn_new, vn_new, kp, vp, pos0, ts):
    n_pages = len(kp)
    nrow = qr.shape[0]
    ppb = MOBA_BLOCK // PAGE_SIZE
    n_past = n_pages // ppb
    blk_lane = lax.broadcasted_iota(jnp.int32, (BRANCH_W, LANES), 1)
    kmean_t = jnp.zeros((BRANCH_W, LANES), F32)
    for n in range(n_past):
        acc = kp[n * ppb][...]
        for r in range(1, ppb):
            acc = acc + kp[n * ppb + r][...]
        kmean_t = jnp.where(blk_lane == n, jnp.sum(acc, axis=1, keepdims=True) * (1.0 / MOBA_BLOCK), kmean_t)
    gate = jnp.dot(qr, kmean_t, precision=lax.Precision.HIGHEST, preferred_element_type=F32)
    bias = _top_blocks_bias(gate, n_past)
    qb = (qr * (A_DH ** -0.5)).astype(BF)
    scores = []
    for p in range(n_pages):
        s = jnp.dot(qb, kp[p][...].astype(BF), preferred_element_type=F32)
        scores.append(s + bias[:, p // ppb:p // ppb + 1])
    pad = jnp.zeros((LANES - kn_new.shape[0], BRANCH_W), F32)
    kn = jnp.concatenate([kn_new, pad], axis=0).astype(BF)
    vn = jnp.concatenate([vn_new, pad], axis=0).astype(BF)
    s_own = lax.dot_general(qb, kn, NT, preferred_element_type=F32)
    qi = lax.broadcasted_iota(jnp.int32, (nrow, LANES), 0) // A_HEADS
    kj = lax.broadcasted_iota(jnp.int32, (nrow, LANES), 1)
    own_blk = pos0 // MOBA_BLOCK
    ok = (kj < ts) & (own_blk * MOBA_BLOCK + kj <= pos0 + qi)
    s_own = jnp.where(ok, s_own, NEG_INF)
    m = jnp.max(s_own, axis=1, keepdims=True)
    for s in scores:
        m = jnp.maximum(m, jnp.max(s, axis=1, keepdims=True))
    p_own = jnp.exp(s_own - m)
    l = jnp.sum(p_own, axis=1, keepdims=True)
    out = jnp.dot(p_own.astype(BF), vn, preferred_element_type=F32)
    for p in range(n_pages):
        pr = jnp.exp(scores[p] - m)
        l = l + jnp.sum(pr, axis=1, keepdims=True)
        out = out + lax.dot_general(pr.astype(BF), vp[p][...].astype(BF), NT, preferred_element_type=F32)
    out = out / l
    hrow = lax.broadcasted_iota(jnp.int32, (nrow, BRANCH_W), 0) % A_HEADS
    hlane = lax.broadcasted_iota(jnp.int32, (nrow, BRANCH_W), 1) // A_DH
    out = jnp.where(hrow == hlane, out, 0.0)
    return jnp.sum(out.reshape(ts, A_HEADS, BRANCH_W), axis=1)


def _sample_attention(qrows, kn, vn, cache_k, cache_v, page_table, layer, pos0, ts):
    Bd, n_pages = page_table.shape
    nrow = qrows.shape[1]
    rps = REQS_PER_STEP
    assert Bd % rps == 0
    page = lambda r, p: pl.BlockSpec((None, None, BRANCH_W, PAGE_SIZE),
                                     lambda g, pt, r=r, p=p: (layer, pt[(g * rps + r) * n_pages + p], 0, 0))
    req = lambda rows: pl.BlockSpec((rps, rows, BRANCH_W), lambda g, pt: (g, 0, 0))
    pages = [page(r, p) for r in range(rps) for _ in range(2) for p in range(n_pages)]
    caches = [c for _ in range(rps) for c in ([cache_k] * n_pages + [cache_v] * n_pages)]
    grid_spec = pltpu.PrefetchScalarGridSpec(
        num_scalar_prefetch=1,
        grid=(Bd // rps,),
        in_specs=[req(nrow), req(kn.shape[1]), req(vn.shape[1])] + pages,
        out_specs=req(ts),
    )
    return pl.pallas_call(
        functools.partial(_sattn_body, n_pages=n_pages, pos0=pos0, ts=ts),
        grid_spec=grid_spec,
        out_shape=jax.ShapeDtypeStruct((Bd, ts, BRANCH_W), F32),
        compiler_params=_cp(("parallel",)),
        name="moba_sample",
    )(page_table.reshape(-1), qrows, kn, vn, *caches)


def _smix_body(uf_ref, db_ref, dc_ref, dh_ref, zb_ref, pw_ref, ps_ref, sw_ref, yb_ref, yd_ref, zn_ref, *, bd, ts, pos0):
    ps, sw = ps_ref[...], sw_ref[...]
    slab = lambda ref, r: ref[r * bd:(r + 1) * bd, :]
    z = [slab(zb_ref, r) for r in range(SCONV_W - 1)] + [slab(dc_ref, t) * slab(dh_ref, t) for t in range(ts)]
    for t in range(ts):
        u = slab(uf_ref, POOL_BUF + t)
        for g, w in enumerate(POOL_WINDOWS):
            sl = slice(g * POOL_GC, (g + 1) * POOL_GC)
            win = u[:, sl]
            for r in range(1, w):
                win = win + slab(uf_ref, POOL_BUF + t - r)[:, sl]
            pooled = win / float(min(w, pos0 + t + 1)) - u[:, sl]
            y = jnp.dot(pooled.astype(BF), pw_ref[g], preferred_element_type=F32) * ps[:, sl]
            yb_ref[t * bd:(t + 1) * bd, sl] = y.astype(yb_ref.dtype)
        conv = sw[0:1] * z[t] + sw[1:2] * z[t + 1] + sw[2:3] * z[t + 2]
        yd_ref[t * bd:(t + 1) * bd, :] = (slab(db_ref, t) * conv).astype(yd_ref.dtype)
    for r in range(SCONV_W - 1):
        zn_ref[r * bd:(r + 1) * bd, :] = z[ts + r]


def _sample_mixers(ufull, proj, zbuf, pool_w_bf, pool_scale, sconv_w, Bd, Ts, pos0):
    M = Ts * Bd
    whole = lambda a: pl.BlockSpec(a.shape, lambda i, n=a.ndim: (0,) * n)
    blk = lambda cb: pl.BlockSpec((M, BRANCH_W), lambda i, cb=cb: (0, cb))
    out = lambda r, dt: (pl.BlockSpec((r, BRANCH_W), lambda i: (0, 0)), jax.ShapeDtypeStruct((r, BRANCH_W), dt))
    outs = [out(M, BF), out(M, BF), out((SCONV_W - 1) * Bd, F32)]
    return pl.pallas_call(
        functools.partial(_smix_body, bd=Bd, ts=Ts, pos0=pos0),
        grid=(1,),
        in_specs=[whole(ufull), blk(CB_DB), blk(CB_DC), blk(CB_DH), whole(zbuf), whole(pool_w_bf), whole(pool_scale),
                  whole(sconv_w)],
        out_specs=[o[0] for o in outs],
        out_shape=[o[1] for o in outs],
        compiler_params=_cp(("arbitrary",)),
        name="pool_sconv_sample",
    )(ufull, proj, proj, proj, zbuf, pool_w_bf, pool_scale, sconv_w)


def _shgrn_body(cq_ref, cf_ref, ci_ref, cg_ref, s0_ref, lb_ref, ng_ref, yc_ref, so_ref, *, ts):
    lb, ng = lb_ref[...], ng_ref[...]
    for r in range(cq_ref.shape[0]):
        for h in range(C_HEADS):
            sl = slice(h * C_DK, (h + 1) * C_DK)
            o, st = _gla_chunk(cq_ref[r, :, sl], cf_ref[r, :, sl], ci_ref[r, :, sl], lb[:, sl], s0_ref[r, h].T, ts)
            so_ref[r, h] = st.T
            yc_ref[r, :, sl] = _hgrn_out(o, cg_ref[r, :, sl], ng[:, sl])


def _sample_hgrn(cq, cf, ci, cg, s0, lb, ng, ts):
    Bd, rows, _ = cq.shape
    rps = math.gcd(Bd, 4)
    req = pl.BlockSpec((rps, rows, BRANCH_W), lambda b: (b, 0, 0))
    st = pl.BlockSpec((rps, C_HEADS, C_DK, C_DK), lambda b: (b, 0, 0, 0))
    vec = pl.BlockSpec((1, BRANCH_W), lambda b: (0, 0))
    return pl.pallas_call(
        functools.partial(_shgrn_body, ts=ts),
        grid=(Bd // rps,),
        in_specs=[req, req, req, req, st, vec, vec],
        out_specs=[req, st],
        out_shape=[jax.ShapeDtypeStruct((Bd, rows, BRANCH_W), F32), jax.ShapeDtypeStruct(s0.shape, F32)],
        compiler_params=_cp(("parallel",)),
        name="hgrn_sample",
    )(cq, cf, ci, cg, s0, lb, ng)


def _prompt_layer(x, B, T, w, lb, tables):
    tm = min(T, 1024)
    proj = _matmul(x, w["w_in"], tm, N_IN // 4, "proj_prompt")
    kt, vtf, kb, qt, vt, bias = _attn_prep(proj, tables, B, T)
    ya = _prompt_attention(qt, kb, vt, bias, B, T)
    yb, yd, ztail = _prompt_mixers(proj, w["pool_w"], w["pool_scale"], w["sconv_w"], B, T, min(T, 512))
    yc, s_new = _prompt_hgrn(proj, lb, w["hgrn_norm_g"], B, T, min(T, 512))
    h = _merge((ya, yb, yc, yd), proj, x, w["w_branch"], w["w_o"], w["ln1_g"], w["ln1_b"], 256)
    act, up_tail = _prompt_ffn_up(h, w["ffn_up"], w["ffn_conv"], B, T, 256)
    out = _prompt_ffn_down(act, w["ffn_down"], h, w["ln2_g"], w["ln2_b"], 512)
    proj3 = proj.reshape(B, T, N_IN)
    k, v = kt.reshape(B, A_HEADS, A_DH, T), vtf.reshape(B, A_HEADS, A_DH, T)
    new_pool = proj3[:, T - POOL_BUF:, CB_PU * BRANCH_W:(CB_PU + 1) * BRANCH_W]
    new_sconv = ztail[:, 8 - (SCONV_W - 1):]
    new_ffn = up_tail[:, 8 - (FFN_CONV_W - 1):]
    return out, k, v, new_pool, s_new, new_sconv, new_ffn


def _to_tm(a):
    return a.transpose(1, 0, 2).reshape(a.shape[1] * a.shape[0], a.shape[2])


def _from_tm(a, Bd):
    return a.reshape(a.shape[0] // Bd, Bd, a.shape[1]).transpose(1, 0, 2)


def _sample_layer(x, Bd, Ts, pos0, layer, w, lb, tables, cache_k, cache_v, page_table, pool_buf, hgrn_s, sconv_buf, ffn_buf):
    M = Ts * Bd
    proj = _matmul(x, w["w_in"], M, BRANCH_W, "proj_sample")
    qf, _, kf, _, _ = _rope(proj, tables, M)
    col = lambda cb: proj[:, cb * BRANCH_W:(cb + 1) * BRANCH_W]
    q_r, k_r, v_r = _from_tm(qf, Bd), _from_tm(kf, Bd), _from_tm(col(CB_AV), Bd)
    head_of_lane = jnp.arange(BRANCH_W) // A_DH
    hmask = (head_of_lane[None, :] == jnp.arange(A_HEADS)[:, None]).astype(F32)
    qrows = (q_r[:, :, None, :] * hmask[None, None]).reshape(Bd, Ts * A_HEADS, BRANCH_W)
    pad8 = lambda a: jnp.pad(a, ((0, 0), (0, 8 - Ts), (0, 0)))
    ya = _sample_attention(qrows, pad8(k_r), pad8(v_r), cache_k, cache_v, page_table, layer, pos0, Ts)
    ufull = jnp.concatenate([_to_tm(pool_buf), col(CB_PU)], axis=0)
    yb, yd, znew = _sample_mixers(ufull, proj, _to_tm(sconv_buf), w["pool_w"], w["pool_scale"], w["sconv_w"], Bd, Ts, pos0)
    req8 = lambda cb: pad8(_from_tm(col(cb), Bd))
    yc, s_new = _sample_hgrn(req8(CB_CQ), req8(CB_CF), req8(CB_CI), req8(CB_CG), hgrn_s, lb, w["hgrn_norm_g"], Ts)
    ys = (_to_tm(ya), yb, _to_tm(yc[:, :Ts]), yd)
    h = _merge(ys, proj, x, w["w_branch"], w["w_o"], w["ln1_g"], w["ln1_b"], min(M, 256))
    up = _matmul(h, w["ffn_up"], M, BRANCH_W, "ffn_up_sample")
    full = jnp.concatenate([_to_tm(ffn_buf), up], axis=0)
    out = _sample_ffn(full, w["ffn_conv"], w["ffn_down"], h, w["ln2_g"], w["ln2_b"], Bd, Ts)
    k = k_r.reshape(Bd, Ts, A_HEADS, A_DH)
    v = v_r.reshape(Bd, Ts, A_HEADS, A_DH)
    new_pool = _from_tm(ufull[Ts * Bd:], Bd)
    new_sconv = _from_tm(znew, Bd)
    new_ffn = _from_tm(full[Ts * Bd:], Bd)
    return out, k, v, new_pool, s_new, new_sconv, new_ffn


def _layer_weights(l, w_in, w_branch, w_o, pool_w, pool_scale, hgrn_norm_g, sconv_w, ln1_g, ln1_b, ffn_up, ffn_conv,
                   ffn_down, ln2_g, ln2_b):
    row = lambda a: a[l][None, :]
    return {
        "w_in": jnp.concatenate([w_in[l][:, c * BRANCH_W:(c + 1) * BRANCH_W] for c in COLUMN_BLOCK_ORDER], axis=1).astype(BF),
        "w_branch": w_branch[l].astype(BF), "w_o": w_o[l].astype(BF),
        "pool_w": pool_w[l].astype(BF), "pool_scale": row(pool_scale), "hgrn_norm_g": row(hgrn_norm_g),
        "sconv_w": sconv_w[l], "ln1_g": row(ln1_g), "ln1_b": row(ln1_b), "ffn_up": ffn_up[l].astype(BF),
        "ffn_conv": ffn_conv[l], "ffn_down": ffn_down[l].astype(BF), "ln2_g": row(ln2_g), "ln2_b": row(ln2_b),
    }


def kernel(x_prompt, x_sample, cache_k, cache_v, state_pool, state_hgrn, state_sconv, state_ffn, page_table, w_in,
           w_branch, w_o, pool_w, pool_scale, hgrn_lb_logits, hgrn_norm_g, sconv_w, ln1_g, ln1_b, ffn_up, ffn_conv,
           ffn_down, ln2_g, ln2_b):
    depth = w_in.shape[0]
    Bp, T, _ = x_prompt.shape
    Bd, Ts, _ = x_sample.shape
    n_pages = page_table.shape[1]
    pos0 = n_pages * PAGE_SIZE
    assert T % MOBA_BLOCK == 0 and pos0 % MOBA_BLOCK == 0 and Ts <= 8 and T >= HALO
    lb_all = _lower_bounds(hgrn_lb_logits.astype(F32))
    tab_p = _rope_tables(jnp.arange(T))
    tab_s = _rope_tables(pos0 + jnp.repeat(jnp.arange(Ts), Bd))
    ck = cache_k.transpose(0, 1, 3, 4, 2).reshape(depth, cache_k.shape[1], BRANCH_W, PAGE_SIZE)
    cv = cache_v.transpose(0, 1, 3, 4, 2).reshape(depth, cache_v.shape[1], BRANCH_W, PAGE_SIZE)
    hp = x_prompt.reshape(Bp * T, D_MODEL)
    hs = _to_tm(x_sample)
    outs_p, outs_s = [], []
    for l in range(depth):
        w = _layer_weights(l, w_in, w_branch, w_o, pool_w, pool_scale, hgrn_norm_g, sconv_w, ln1_g, ln1_b, ffn_up,
                           ffn_conv, ffn_down, ln2_g, ln2_b)
        lb = lb_all[l][None, :]
        rp = _prompt_layer(hp, Bp, T, w, lb, tab_p)
        rs = _sample_layer(hs, Bd, Ts, pos0, l, w, lb, tab_s, ck, cv, page_table, state_pool[l], state_hgrn[l],
                           state_sconv[l], state_ffn[l])
        hp, hs = rp[0], rs[0]
        outs_p.append(rp[1:])
        outs_s.append(rs[1:])
    stack = lambda outs: tuple(jnp.stack([o[j] for o in outs]) for j in range(6))
    sp = stack(outs_p)
    tokens_first = lambda a: a.transpose(0, 1, 4, 2, 3)
    sp = (tokens_first(sp[0]), tokens_first(sp[1])) + sp[2:]
    return (hp.reshape(Bp, T, D_MODEL), _from_tm(hs, Bd)) + sp + stack(outs_s)
```

```python
import functools
import math

import jax
import jax.numpy as jnp
import numpy as np
from jax import lax
from jax.experimental import pallas as pl
from jax.experimental.pallas import tpu as pltpu

F32 = jnp.float32
BF = jnp.bfloat16
NEG_INF = float("-inf")
LOG2_E = math.log2(math.e)

D_MODEL = 1024
BRANCH_W = 512
N_BRANCH = 4
A_HEADS = 8
A_DH = 64
MOBA_BLOCK = 256
MOBA_TOPK = 3
ROPE_DIM = A_DH // 4
ROPE_THETA = 500000.0
POOL_WINDOWS = (2, 4, 8, 16)
POOL_GC = 128
POOL_BUF = 15
C_HEADS = 4
C_DK = 128
SCONV_W = 3
D_FF = 2816
FFN_CONV_W = 3
PAGE_SIZE = 128
DEPTH = 4
DN_ALPHA = (2 * DEPTH) ** 0.25
LN_EPS = 1e-5
RMS_EPS = 1e-6

CB_AQ, CB_AK, CB_AV, CB_PU, CB_CQ, CB_CF, CB_CI, CB_CG, CB_DB, CB_DC, CB_DH = range(11)
N_MIX = 11 * BRANCH_W
N_GATE = N_BRANCH * D_MODEL

LANES = 128
HALO = 16
SUM_ROWS = 16
VMEM_LIMIT = 56 * 1024 * 1024

NT = (((1,), (1,)), ((), ()))
TN = (((0,), (0,)), ((), ()))


def _cp(sem, vmem=VMEM_LIMIT):
    return pltpu.CompilerParams(dimension_semantics=sem, vmem_limit_bytes=vmem)


def _sigmoid(x):
    return 1.0 / (1.0 + jnp.exp(-x))


def _silu(x):
    return x * _sigmoid(x)


def _layer_norm(x, g, b):
    mu = jnp.mean(x, axis=-1, keepdims=True)
    xc = x - mu
    var = jnp.mean(xc * xc, axis=-1, keepdims=True)
    return xc * lax.rsqrt(var + LN_EPS) * g + b


def _lb_body(logit_ref, o_ref):
    x = logit_ref[...]
    m = jnp.max(x, axis=0, keepdims=True)
    e = jnp.exp(x - m)
    p = e / jnp.sum(e, axis=0, keepdims=True)
    acc = jnp.zeros_like(p[0:1])
    rows = []
    for l in range(x.shape[0]):
        acc = acc + p[l:l + 1]
        rows.append(acc - p[0:1])
    o_ref[...] = jnp.concatenate(rows, axis=0)


def _lower_bounds(logits):
    return pl.pallas_call(_lb_body, out_shape=jax.ShapeDtypeStruct(logits.shape, F32), name="hgrn_lb")(logits)


def _mm_body(x_ref, w_ref, o_ref, xb_ref, *, gate):
    @pl.when(pl.program_id(1) == 0)
    def _():
        xb_ref[...] = x_ref[...].astype(BF)

    y = jnp.dot(xb_ref[...], w_ref[...], preferred_element_type=F32)
    o_ref[...] = (_sigmoid(y) if gate else y).astype(o_ref.dtype)


def _matmul(x, w_bf, tm, tn, name, gate=False):
    M, K = x.shape
    N = w_bf.shape[1]
    assert M % tm == 0 and N % tn == 0
    return pl.pallas_call(
        functools.partial(_mm_body, gate=gate),
        grid=(M // tm, N // tn),
        in_specs=[pl.BlockSpec((tm, K), lambda i, j: (i, 0)),
                  pl.BlockSpec((K, tn), lambda i, j: (0, j))],
        out_specs=pl.BlockSpec((tm, tn), lambda i, j: (i, j)),
        out_shape=jax.ShapeDtypeStruct((M, N), BF if gate else F32),
        scratch_shapes=[pltpu.VMEM((tm, K), BF)],
        compiler_params=_cp(("parallel", "arbitrary")),
        name=name,
    )(x, w_bf)


def _rope_tables(pos):
    half = ROPE_DIM // 2
    inv = jnp.power(ROPE_THETA, -jnp.arange(0, ROPE_DIM, 2, dtype=F32) / ROPE_DIM)
    ang = pos.astype(F32)[:, None] * inv[None, :]
    cos, sin = jnp.cos(ang), jnp.sin(ang)
    n = pos.shape[0]
    ones = jnp.ones((n, A_DH - ROPE_DIM), F32)
    zeros = jnp.zeros((n, A_DH - ROPE_DIM), F32)
    zh = jnp.zeros((n, half), F32)
    c = jnp.concatenate([cos, cos, ones], axis=1)
    s_lo = jnp.concatenate([-sin, zh, zeros], axis=1)
    s_hi = jnp.concatenate([zh, sin, zeros], axis=1)
    tile = lambda a: jnp.tile(a, (1, A_HEADS))
    return tile(c), tile(s_lo), tile(s_hi)


def _rope_body(q_ref, k_ref, v_ref, c_ref, sl_ref, sh_ref, qf_ref, qb_ref, kf_ref, kb_ref, vb_ref):
    c, sl, sh = c_ref[...], sl_ref[...], sh_ref[...]
    half = ROPE_DIM // 2
    w = q_ref.shape[1]

    def rope(x):
        return x * c + pltpu.roll(x, w - half, 1) * sl + pltpu.roll(x, half, 1) * sh

    q = rope(q_ref[...])
    k = rope(k_ref[...])
    qf_ref[...] = q
    qb_ref[...] = (q * (A_DH ** -0.5)).astype(BF)
    kf_ref[...] = k
    kb_ref[...] = k.astype(BF)
    vb_ref[...] = v_ref[...].astype(BF)


def _rope(proj, tables, tm):
    M = proj.shape[0]
    ntab = tables[0].shape[0] // tm
    blk = lambda cb: pl.BlockSpec((tm, BRANCH_W), lambda i, cb=cb: (i, cb))
    tab = pl.BlockSpec((tm, BRANCH_W), lambda i: (i % ntab, 0))
    out = pl.BlockSpec((tm, BRANCH_W), lambda i: (i, 0))
    sds = lambda dt: jax.ShapeDtypeStruct((M, BRANCH_W), dt)
    return pl.pallas_call(
        _rope_body,
        grid=(M // tm,),
        in_specs=[blk(CB_AQ), blk(CB_AK), blk(CB_AV), tab, tab, tab],
        out_specs=[out] * 5,
        out_shape=[sds(F32), sds(BF), sds(F32), sds(BF), sds(BF)],
        compiler_params=_cp(("parallel",)),
        name="rope",
    )(proj, proj, proj, *tables)


def _top_blocks_bias(gate, n_past):
    col = lax.broadcasted_iota(jnp.int32, gate.shape, 1)
    g = jnp.where(col < n_past, gate, NEG_INF)
    bias = jnp.full(gate.shape, NEG_INF, F32)
    for _ in range(MOBA_TOPK):
        m = jnp.max(g, axis=1, keepdims=True)
        idx = jnp.min(jnp.where(g == m, col, LANES), axis=1, keepdims=True)
        pick = (col == idx) & (m > NEG_INF)
        bias = jnp.where(pick, 0.0, bias)
        g = jnp.where(pick, NEG_INF, g)
    return bias


def _split_bf16(x):
    hi = x.astype(BF)
    return hi, (x - hi.astype(F32)).astype(BF)


def _prep_body(qkv_ref, c_ref, sl_ref, sh_ref, kt_ref, vtf_ref, kb_ref, qt_ref, vt_ref, bias_ref, km_ref):
    i = pl.program_id(1)
    nbp = km_ref.shape[0]
    q_ref, k_ref, v_ref = (qkv_ref.at[:, n * BRANCH_W:(n + 1) * BRANCH_W] for n in range(3))

    @pl.when(i == 0)
    def _():
        km_ref[...] = jnp.zeros_like(km_ref)

    c, sl, sh = c_ref[...], sl_ref[...], sh_ref[...]
    half = ROPE_DIM // 2

    def rope(x):
        return x * c + pltpu.roll(x, BRANCH_W - half, 1) * sl + pltpu.roll(x, half, 1) * sh

    q = rope(q_ref[...])
    k = rope(k_ref[...])
    kt_ref[...] = k.T
    kb_ref[...] = k.astype(BF)
    qt_ref[...] = (q * (A_DH ** -0.5 * LOG2_E)).T.astype(BF)
    v_t = v_ref[...].T
    vtf_ref[...] = v_t
    vt_ref[...] = v_t.astype(BF)

    km = km_ref[...]
    lane = lax.broadcasted_iota(jnp.int32, km.shape, 1)
    stack = jnp.concatenate([jnp.where(lane // A_DH == h, km, 0.0) for h in range(A_HEADS)], axis=0)
    k_hi, k_lo = _split_bf16(stack)
    q_hi, q_lo = _split_bf16(q)
    dot = lambda a, b: lax.dot_general(a, b, NT, preferred_element_type=F32)
    gate = (dot(k_hi, q_hi) + dot(k_lo, q_hi) + dot(k_hi, q_lo)).reshape(A_HEADS, nbp, MOBA_BLOCK)
    blk = lax.broadcasted_iota(jnp.int32, gate.shape, 1)
    gate = jnp.where(blk < i, gate, NEG_INF)
    bias = jnp.full(gate.shape, NEG_INF, F32)
    for _ in range(MOBA_TOPK):
        m = jnp.max(gate, axis=1, keepdims=True)
        idx = jnp.min(jnp.where(gate == m, blk, nbp), axis=1, keepdims=True)
        pick = (blk == idx) & (m > NEG_INF)
        bias = jnp.where(pick, 0.0, bias)
        gate = jnp.where(pick, NEG_INF, gate)
    bias_ref[0] = bias.reshape(A_HEADS * nbp, MOBA_BLOCK)
    km_ref[pl.ds(i, 1), :] = jnp.sum(k, axis=0, keepdims=True) * (1.0 / MOBA_BLOCK)


def _layer_slot(layer, depth, shape, block, index_map):
    spec = pl.BlockSpec((None,) + block, lambda *a: (layer,) + tuple(index_map(*a)))
    return spec, jax.ShapeDtypeStruct((depth,) + shape, F32)


def _attn_prep(proj, tables, B, T, layer, depth, kv_carried):
    nq = T // MOBA_BLOCK
    nbp = -(-nq // 8) * 8
    tq = MOBA_BLOCK
    assert (CB_AQ, CB_AK, CB_AV) == tuple(range(CB_AQ, CB_AQ + 3)) and CB_AQ % 3 == 0
    qkv = pl.BlockSpec((tq, 3 * BRANCH_W), lambda b, i: (b * nq + i, CB_AQ // 3))
    tab = pl.BlockSpec((tq, BRANCH_W), lambda b, i: (i, 0))
    row = pl.BlockSpec((tq, BRANCH_W), lambda b, i: (b * nq + i, 0))
    tr = pl.BlockSpec((BRANCH_W, tq), lambda b, i: (b, i))
    slot, slot_shape = _layer_slot(layer, depth, (B * BRANCH_W, T), (BRANCH_W, tq), lambda b, i: (b, i))
    carried = [] if kv_carried is None else list(kv_carried)
    n_in = 4
    body = _prep_body if not carried else (lambda *refs: _prep_body(*refs[:n_in], *refs[n_in + len(carried):]))
    return pl.pallas_call(
        body,
        grid=(B, nq),
        in_specs=[qkv, tab, tab, tab] + [pl.BlockSpec(memory_space=pl.ANY)] * len(carried),
        out_specs=[slot, slot, row, tr, tr, pl.BlockSpec((1, A_HEADS * nbp, tq), lambda b, i: (b, 0, i))],
        out_shape=[slot_shape, slot_shape,
                   jax.ShapeDtypeStruct((B * T, BRANCH_W), BF),
                   jax.ShapeDtypeStruct((B * BRANCH_W, T), BF), jax.ShapeDtypeStruct((B * BRANCH_W, T), BF),
                   jax.ShapeDtypeStruct((B, A_HEADS * nbp, T), F32)],
        input_output_aliases={n_in + n: n for n in range(len(carried))},
        scratch_shapes=[pltpu.VMEM((nbp, BRANCH_W), F32)],
        compiler_params=_cp(("parallel", "arbitrary")),
        name="attn_prep",
    )(proj, *tables, *carried)


def _attn_body(qt_ref, k_ref, vt_ref, bias_ref, o_ref, sa_ref, sb_ref, pp_ref):
    i = pl.program_id(2)
    tq = MOBA_BLOCK
    npair = LANES // A_DH
    nbp = bias_ref.shape[1] // npair
    qt = qt_ref[...]
    rowi = lax.broadcasted_iota(jnp.int32, qt.shape, 0)
    qts = [jnp.where((rowi >= e * A_DH) & (rowi < (e + 1) * A_DH), qt, jnp.zeros_like(qt)) for e in range(npair)]
    block = lambda j: pl.ds(j * tq if isinstance(j, int) else pl.multiple_of(j * tq, tq), tq)
    ones = jnp.ones((SUM_ROWS, tq), BF)

    def scores(s_ref, j):
        kj = k_ref[block(j), :]
        for e in range(npair):
            s_ref[e] = jnp.dot(kj, qts[e], preferred_element_type=F32)

    def pending_product(e, jp):
        vt = jnp.concatenate([vt_ref[e * A_DH:(e + 1) * A_DH, block(jp)], ones], axis=0)
        return jnp.dot(vt, pp_ref[e], preferred_element_type=F32)

    def settle(e, head, jp):
        m, l, acc, pa = head
        pv = pending_product(e, jp)
        return m, pa * l + pv[A_DH:A_DH + 1], pa * acc + pv[:A_DH]

    def consume(s_ref, j, carry, own):
        heads, jp = carry
        new = []
        for e in range(npair):
            m, l, acc = settle(e, heads[e], jp)
            s = s_ref[e]
            if own:
                causal = (lax.broadcasted_iota(jnp.int32, (tq, tq), 0) <= lax.broadcasted_iota(jnp.int32, (tq, tq), 1))
                s = jnp.where(causal, s, NEG_INF)
                mn = jnp.maximum(m, jnp.max(s, axis=0, keepdims=True))
                shift = mn
            else:
                b = bias_ref[0, pl.ds(e * nbp + j, 1), :]
                mn = jnp.maximum(m, jnp.max(s, axis=0, keepdims=True) + b)
                shift = mn - b
            pp_ref[e] = jnp.exp2(s - shift).astype(BF)
            new.append((mn, l, acc, jnp.exp2(m - mn)))
        return tuple(new), j

    def two_blocks(p, carry):
        scores(sb_ref, 2 * p + 1)
        carry = consume(sa_ref, 2 * p, carry, own=False)
        scores(sa_ref, 2 * p + 2)
        return consume(sb_ref, 2 * p + 1, carry, own=False)

    def odd_tail(carry):
        scores(sb_ref, i)
        carry = consume(sa_ref, i - 1, carry, own=False)
        return consume(sb_ref, i, carry, own=True)

    def even_tail(carry):
        return consume(sa_ref, i, carry, own=True)

    pp_ref[...] = jnp.zeros_like(pp_ref)
    init = tuple((jnp.full((1, tq), -1e30, F32), jnp.zeros((1, tq), F32), jnp.zeros((A_DH, tq), F32),
                  jnp.ones((1, tq), F32)) for _ in range(npair))
    scores(sa_ref, 0)
    state = lax.fori_loop(0, i // 2, two_blocks, (init, jnp.int32(0)))
    heads, jp = lax.cond(i % 2 == 1, odd_tail, even_tail, state)
    outs = []
    for e in range(npair):
        _, l, acc = settle(e, heads[e], jp)
        outs.append(acc / l)
    o_ref[...] = jnp.concatenate(outs, axis=0).T.astype(o_ref.dtype)


def _prompt_attention(qt, kb, vt, bias, B, T):
    nq = T // MOBA_BLOCK
    npair = LANES // A_DH
    ngrp = BRANCH_W // LANES
    nbp = bias.shape[1] // A_HEADS
    return pl.pallas_call(
        _attn_body,
        grid=(B, ngrp, nq),
        in_specs=[pl.BlockSpec((LANES, MOBA_BLOCK), lambda b, hp, i: (b * ngrp + hp, i)),
                  pl.BlockSpec((T, LANES), lambda b, hp, i: (b, hp)),
                  pl.BlockSpec((LANES, T), lambda b, hp, i: (b * ngrp + hp, 0)),
                  pl.BlockSpec((1, npair * nbp, MOBA_BLOCK), lambda b, hp, i: (b, hp, i))],
        out_specs=pl.BlockSpec((MOBA_BLOCK, LANES), lambda b, hp, i: (b * nq + i, hp)),
        out_shape=jax.ShapeDtypeStruct((B * T, BRANCH_W), BF),
        scratch_shapes=[pltpu.VMEM((npair, MOBA_BLOCK, MOBA_BLOCK), F32)] * 2
        + [pltpu.VMEM((npair, MOBA_BLOCK, MOBA_BLOCK), BF)],
        compiler_params=_cp(("parallel", "parallel", "arbitrary")),
        name="moba_prompt",
    )(qt, kb, vt, bias)


def _pool_group(ext_ref, u, g, w, t_first, pw, ps):
    tm = u.shape[0]
    sl = slice(g * POOL_GC, (g + 1) * POOL_GC)
    win = ext_ref[HALO:HALO + tm, sl]
    for r in range(1, w):
        win = win + ext_ref[HALO - r:HALO - r + tm, sl]
    t = t_first + lax.broadcasted_iota(jnp.int32, (tm, POOL_GC), 0)
    cnt = jnp.minimum(w, t + 1).astype(F32)
    pooled = win / cnt - u[:, sl]
    return jnp.dot(pooled.astype(BF), pw, preferred_element_type=F32) * ps[:, sl]


def _mix_body(u_ref, uh_ref, db_ref, dc_ref, dh_ref, dch_ref, dhh_ref, pw_ref, ps_ref, sw_ref,
              yb_ref, yd_ref, zt_ref, ext_ref, zext_ref):
    i = pl.program_id(1)
    tm = u_ref.shape[0]
    first = i == 0
    u = u_ref[...]
    ext_ref[0:HALO, :] = jnp.where(first, 0.0, uh_ref[...])
    ext_ref[HALO:, :] = u
    ps = ps_ref[...]
    for g, w in enumerate(POOL_WINDOWS):
        y = _pool_group(ext_ref, u, g, w, i * tm, pw_ref[g], ps)
        yb_ref[:, g * POOL_GC:(g + 1) * POOL_GC] = y.astype(yb_ref.dtype)
    z = dc_ref[...] * dh_ref[...]
    zext_ref[0:8, :] = jnp.where(first, 0.0, dch_ref[...] * dhh_ref[...])
    zext_ref[8:, :] = z
    sw = sw_ref[...]
    conv = sw[0:1] * zext_ref[6:6 + tm, :] + sw[1:2] * zext_ref[7:7 + tm, :] + sw[2:3] * z
    yd_ref[...] = (db_ref[...] * conv).astype(yd_ref.dtype)
    zt_ref[0] = z[tm - 8:tm]


def _prompt_mixers(proj, pool_w_bf, pool_scale, sconv_w, B, T, tm):
    nt = T // tm
    row = lambda b, i: b * nt + i
    blk = lambda cb: pl.BlockSpec((tm, BRANCH_W), lambda b, i, cb=cb: (row(b, i), cb))

    def halo(cb, h):
        return pl.BlockSpec((h, BRANCH_W), lambda b, i, cb=cb, h=h: (jnp.maximum(row(b, i) * (tm // h) - 1, 0), cb))

    const = lambda shape: pl.BlockSpec(shape, lambda b, i, n=len(shape): (0,) * n)
    return pl.pallas_call(
        _mix_body,
        grid=(B, nt),
        in_specs=[blk(CB_PU), halo(CB_PU, HALO), blk(CB_DB), blk(CB_DC), blk(CB_DH), halo(CB_DC, 8), halo(CB_DH, 8),
                  const((len(POOL_WINDOWS), POOL_GC, POOL_GC)), const((1, BRANCH_W)), const((SCONV_W, BRANCH_W))],
        out_specs=[pl.BlockSpec((tm, BRANCH_W), lambda b, i: (row(b, i), 0)),
                   pl.BlockSpec((tm, BRANCH_W), lambda b, i: (row(b, i), 0)),
                   pl.BlockSpec((1, 8, BRANCH_W), lambda b, i: (b, 0, 0))],
        out_shape=[jax.ShapeDtypeStruct((B * T, BRANCH_W), BF), jax.ShapeDtypeStruct((B * T, BRANCH_W), BF),
                   jax.ShapeDtypeStruct((B, 8, BRANCH_W), F32)],
        scratch_shapes=[pltpu.VMEM((tm + HALO, BRANCH_W), F32), pltpu.VMEM((tm + 8, BRANCH_W), F32)],
        compiler_params=_cp(("parallel", "arbitrary")),
        name="pool_sconv_prompt",
    )(proj, proj, proj, proj, proj, proj, proj, pool_w_bf, pool_scale, sconv_w)


def _gla_chunk(cq, cf, ci, lb, st, n_valid):
    rows = cq.shape[0]
    f = lb + (1.0 - lb) * _sigmoid(cf)
    g = jnp.log(f)
    r_i = lax.broadcasted_iota(jnp.int32, (rows, rows), 0)
    c_i = lax.broadcasted_iota(jnp.int32, (rows, rows), 1)
    tri = jnp.where((r_i >= c_i) & (c_i < n_valid), 1.0, 0.0).astype(F32)
    b = jnp.dot(tri, g, precision=lax.Precision.HIGHEST, preferred_element_type=F32)
    qs = _silu(cq)
    kk = 1.0 - f
    t_i = lax.broadcasted_iota(jnp.int32, (rows, 1), 0)
    o = jnp.zeros((rows, C_DK), F32)
    for s in range(n_valid):
        e = jnp.exp(jnp.minimum(b - b[s:s + 1], 0.0))
        a = jnp.sum(qs * kk[s:s + 1] * e, axis=1, keepdims=True)
        o = o + jnp.where(t_i >= s, a, 0.0) * ci[s:s + 1]
    o = o + lax.dot_general((qs * jnp.exp(b)).astype(BF), st.astype(BF), NT, preferred_element_type=F32)
    bl = b[n_valid - 1:n_valid]
    ks = jnp.where(t_i < n_valid, kk * jnp.exp(jnp.minimum(bl - b, 0.0)), 0.0)
    upd = lax.dot_general(ci.astype(BF), ks.astype(BF), TN, preferred_element_type=F32)
    return o, st * jnp.exp(bl) + upd


def _hgrn_out(o, cg, ng):
    o = o * lax.rsqrt(jnp.mean(o * o, axis=1, keepdims=True) + RMS_EPS)
    return o * ng * _silu(cg)


GROUP = 128
GROUP_HALVES = tuple(GROUP >> (s + 1) for s in range(int(math.log2(GROUP))))


def _decay_sum_matrix():
    t = np.arange(GROUP)[:, None]
    u = np.arange(GROUP)[None, :]
    blocks = [u <= t, u > t]
    for h in GROUP_HALVES:
        r = (t // (2 * h)) * 2 * h + h - 1
        second = (t // h) % 2 == 1
        blocks.append(np.where(second, (u > r) & (u <= t), (u > t) & (u <= r)))
    d = np.concatenate(blocks, axis=0).astype(np.float32)
    return jnp.asarray(np.concatenate([d, d], axis=1), dtype=BF)


def _gla_decays(cf, lb, dmat):
    f = lb + (1.0 - lb) * _sigmoid(cf)
    g_hi, g_lo = _split_bf16(jnp.log(f))
    return f, jnp.dot(dmat, jnp.concatenate([g_hi, g_lo], axis=0), preferred_element_type=F32)


def _gla_group(cq, f, x, ci, st):
    e = jnp.exp(x)
    blk = lambda n: e[n * GROUP:(n + 1) * GROUP]
    qs = _silu(cq)
    kk = 1.0 - f
    v = ci.astype(BF)
    t_i = lax.broadcasted_iota(jnp.int32, (GROUP, GROUP), 0)
    s_i = lax.broadcasted_iota(jnp.int32, (GROUP, GROUP), 1)
    row = lax.broadcasted_iota(jnp.int32, (GROUP, C_DK), 0)
    dot_nt = lambda a, b: lax.dot_general(a.astype(BF), b.astype(BF), NT, preferred_element_type=F32)
    attn = jnp.where(t_i == s_i, dot_nt(qs, kk), 0.0)
    for s, h in enumerate(GROUP_HALVES):
        second = (row // h) % 2 == 1
        scaled_q = jnp.where(second, qs * blk(2 + s), 0.0)
        scaled_k = jnp.where(second, 0.0, kk * blk(2 + s))
        part = dot_nt(scaled_q, scaled_k)
        attn = attn + (part if 2 * h == GROUP else jnp.where(t_i // (2 * h) == s_i // (2 * h), part, 0.0))
    o = jnp.dot(attn.astype(BF), v, preferred_element_type=F32) + dot_nt(qs * blk(0), st)
    upd = lax.dot_general(v, (kk * blk(1)).astype(BF), TN, preferred_element_type=F32)
    return o, st * e[GROUP - 1:GROUP] + upd


def _hgrn_body(c_ref, lb_ref, ng_ref, dm_ref, yc_ref, so_ref, st_ref):
    it = pl.program_id(1)
    tc = c_ref.shape[0]
    cq_ref, cf_ref, ci_ref, cg_ref = (c_ref.at[:, n * BRANCH_W:(n + 1) * BRANCH_W] for n in range(4))

    @pl.when(it == 0)
    def _():
        st_ref[...] = jnp.zeros_like(st_ref)

    lb, ng = lb_ref[...], ng_ref[...]

    def group(gi, carry):
        rows = pl.ds(pl.multiple_of(gi * GROUP, GROUP), GROUP)
        head = lambda h: slice(h * C_DK, (h + 1) * C_DK)
        f, x = _gla_decays(cf_ref[rows, :], lb, dm_ref[...])
        for h in range(C_HEADS):
            sl = head(h)
            o, st = _gla_group(cq_ref[rows, sl], f[:, sl], x[:, sl], ci_ref[rows, sl], st_ref[h])
            st_ref[h] = st
            yc_ref[rows, sl] = _hgrn_out(o, cg_ref[rows, sl], ng[:, sl]).astype(yc_ref.dtype)
        return carry

    lax.fori_loop(0, tc // GROUP, group, 0)

    @pl.when(it == pl.num_programs(1) - 1)
    def _():
        for h in range(C_HEADS):
            so_ref[0, h] = st_ref[h].T


def _prompt_hgrn(proj, lb, ng, B, T, tc):
    nt = T // tc
    dmat = _decay_sum_matrix()
    assert (CB_CQ, CB_CF, CB_CI, CB_CG) == tuple(range(CB_CQ, CB_CQ + 4)) and CB_CQ % 4 == 0
    vec = pl.BlockSpec((1, BRANCH_W), lambda b, i: (0, 0))
    return pl.pallas_call(
        _hgrn_body,
        grid=(B, nt),
        in_specs=[pl.BlockSpec((tc, 4 * BRANCH_W), lambda b, i: (b * nt + i, CB_CQ // 4)), vec, vec,
                  pl.BlockSpec(dmat.shape, lambda b, i: (0, 0))],
        out_specs=[pl.BlockSpec((tc, BRANCH_W), lambda b, i: (b * nt + i, 0)),
                   pl.BlockSpec((1, C_HEADS, C_DK, C_DK), lambda b, i: (b, 0, 0, 0))],
        out_shape=[jax.ShapeDtypeStruct((B * T, BRANCH_W), BF), jax.ShapeDtypeStruct((B, C_HEADS, C_DK, C_DK), F32)],
        scratch_shapes=[pltpu.VMEM((C_HEADS, C_DK, C_DK), F32)],
        compiler_params=_cp(("parallel", "arbitrary")),
        name="hgrn_prompt",
    )(proj, lb, ng, dmat)


def _merge_body(ya_ref, yb_ref, yc_ref, yd_ref, g_ref, x_ref, wb_ref, wo_ref, lg_ref, lbias_ref, h_ref):
    mixed = jnp.zeros(x_ref.shape, F32)
    for n, y_ref in enumerate((ya_ref, yb_ref, yc_ref, yd_ref)):
        br = jnp.dot(y_ref[...].astype(BF), wb_ref[n], preferred_element_type=F32)
        mixed = mixed + g_ref[:, n * D_MODEL:(n + 1) * D_MODEL].astype(F32) * br
    pre = DN_ALPHA * x_ref[...] + jnp.dot(mixed.astype(BF), wo_ref[...], preferred_element_type=F32)
    h_ref[...] = _layer_norm(pre, lg_ref[...], lbias_ref[...])


def _merge(ys, gates, x, wb_bf, wo_bf, ln_g, ln_b, tm):
    M = x.shape[0]
    yspec = pl.BlockSpec((tm, BRANCH_W), lambda i: (i, 0))
    gspec = pl.BlockSpec((tm, N_GATE), lambda i: (i, 0))
    xspec = pl.BlockSpec((tm, D_MODEL), lambda i: (i, 0))
    const = lambda shape: pl.BlockSpec(shape, lambda i, n=len(shape): (0,) * n)
    return pl.pallas_call(
        _merge_body,
        grid=(M // tm,),
        in_specs=[yspec] * N_BRANCH + [gspec, xspec, const((N_BRANCH, BRANCH_W, D_MODEL)), const((D_MODEL, D_MODEL)),
                                       const((1, D_MODEL)), const((1, D_MODEL))],
        out_specs=xspec,
        out_shape=jax.ShapeDtypeStruct((M, D_MODEL), F32),
        compiler_params=_cp(("parallel",)),
        name="merge_ln1",
    )(*ys, gates, x, wb_bf, wo_bf, ln_g, ln_b)


def _ffn_finish(gate, val, wd_ref, h_ref, lg_ref, lb_ref, o_ref):
    a = (_silu(gate) * val).astype(BF)
    pre = DN_ALPHA * h_ref[...] + jnp.dot(a, wd_ref[...], preferred_element_type=F32)
    o_ref[...] = _layer_norm(pre, lg_ref[...], lb_ref[...])


FFN_COLS = 256


def _ffn_up_body(x_ref, w_ref, cw_ref, a_ref, tail_ref, halo_ref, *, tiles_per_seq):
    i = pl.program_id(0)
    tm = x_ref.shape[0]

    @pl.when(i == 0)
    def _():
        halo_ref[...] = jnp.zeros_like(halo_ref)

    xb = x_ref[...].astype(BF)
    seq_start = i % tiles_per_seq == 0

    def conv(cols):
        u = jnp.dot(xb, w_ref[:, cols], preferred_element_type=F32)
        prev = jnp.where(seq_start, 0.0, halo_ref[:, cols])
        halo_ref[:, cols] = u[tm - 8:tm]
        tail_ref[0, :, cols] = u[tm - 8:tm]
        cw = cw_ref[:, cols]
        back1 = jnp.concatenate([prev[7:8], u[:tm - 1]], axis=0)
        back2 = jnp.concatenate([prev[6:8], u[:tm - 2]], axis=0)
        return cw[0:1] * back2 + cw[1:2] * back1 + cw[2:3] * u

    for c in range(D_FF // FFN_COLS):
        gate = conv(slice(c * FFN_COLS, (c + 1) * FFN_COLS))
        val = conv(slice(D_FF + c * FFN_COLS, D_FF + (c + 1) * FFN_COLS))
        a_ref[:, c * FFN_COLS:(c + 1) * FFN_COLS] = (_silu(gate) * val).astype(a_ref.dtype)


def _prompt_ffn_up(h, wu_bf, ffn_conv, B, T, tm):
    M, K = h.shape
    nt = T // tm
    assert D_FF % FFN_COLS == 0
    const = lambda shape: pl.BlockSpec(shape, lambda i, n=len(shape): (0,) * n)
    return pl.pallas_call(
        functools.partial(_ffn_up_body, tiles_per_seq=nt),
        grid=(M // tm,),
        in_specs=[pl.BlockSpec((tm, K), lambda i: (i, 0)), const((K, 2 * D_FF)), const((FFN_CONV_W, 2 * D_FF))],
        out_specs=[pl.BlockSpec((tm, D_FF), lambda i: (i, 0)),
                   pl.BlockSpec((1, 8, 2 * D_FF), lambda i: (i // nt, 0, 0))],
        out_shape=[jax.ShapeDtypeStruct((M, D_FF), BF), jax.ShapeDtypeStruct((B, 8, 2 * D_FF), F32)],
        scratch_shapes=[pltpu.VMEM((8, 2 * D_FF), F32)],
        compiler_params=_cp(("arbitrary",)),
        name="ffn_up_act_prompt",
    )(h, wu_bf, ffn_conv)


def _ffn_down_body(a_ref, wd_ref, h_ref, lg_ref, lb_ref, o_ref):
    pre = DN_ALPHA * h_ref[...] + jnp.dot(a_ref[...], wd_ref[...], preferred_element_type=F32)
    o_ref[...] = _layer_norm(pre, lg_ref[...], lb_ref[...])


def _prompt_ffn_down(a, wd_bf, h, ln_g, ln_b, tm):
    M = h.shape[0]
    const = lambda shape: pl.BlockSpec(shape, lambda i, n=len(shape): (0,) * n)
    xspec = pl.BlockSpec((tm, D_MODEL), lambda i: (i, 0))
    return pl.pallas_call(
        _ffn_down_body,
        grid=(M // tm,),
        in_specs=[pl.BlockSpec((tm, D_FF), lambda i: (i, 0)), const((D_FF, D_MODEL)), xspec,
                  const((1, D_MODEL)), const((1, D_MODEL))],
        out_specs=xspec,
        out_shape=jax.ShapeDtypeStruct((M, D_MODEL), F32),
        compiler_params=_cp(("parallel",)),
        name="ffn_down_prompt",
    )(a, wd_bf, h, ln_g, ln_b)


def _ffn_sample_body(g0_ref, g1_ref, g2_ref, v0_ref, v1_ref, v2_ref, cwg_ref, cwv_ref, wd_ref, h_ref, lg_ref, lb_ref, o_ref):
    cwg, cwv = cwg_ref[...], cwv_ref[...]
    gate = cwg[0:1] * g0_ref[...] + cwg[1:2] * g1_ref[...] + cwg[2:3] * g2_ref[...]
    val = cwv[0:1] * v0_ref[...] + cwv[1:2] * v1_ref[...] + cwv[2:3] * v2_ref[...]
    _ffn_finish(gate, val, wd_ref, h_ref, lg_ref, lb_ref, o_ref)


def _sample_ffn(full, ffn_conv, wd_bf, h, ln_g, ln_b, Bd, Ts):
    blk = lambda c, r: pl.BlockSpec((Bd, D_FF), lambda t, c=c, r=r: (t + r, c))
    cw = lambda c: pl.BlockSpec((FFN_CONV_W, D_FF), lambda t, c=c: (0, c))
    const = lambda shape: pl.BlockSpec(shape, lambda t, n=len(shape): (0,) * n)
    xspec = pl.BlockSpec((Bd, D_MODEL), lambda t: (t, 0))
    return pl.pallas_call(
        _ffn_sample_body,
        grid=(Ts,),
        in_specs=[blk(0, 0), blk(0, 1), blk(0, 2), blk(1, 0), blk(1, 1), blk(1, 2), cw(0), cw(1),
                  const((D_FF, D_MODEL)), xspec, const((1, D_MODEL)), const((1, D_MODEL))],
        out_specs=xspec,
        out_shape=jax.ShapeDtypeStruct((Ts * Bd, D_MODEL), F32),
        compiler_params=_cp(("parallel",)),
        name="ffn_tail_sample",
    )(full, full, full, full, full, full, ffn_conv, ffn_conv, wd_bf, h, ln_g, ln_b)


REQS_PER_STEP = 2


def _sattn_body(pt_ref, q_ref, kn_ref, vn_ref, *rest, n_pages, pos0, ts):
    del pt_ref
    o_ref = rest[2 * n_pages * REQS_PER_STEP]
    for r in range(REQS_PER_STEP):
        pages = rest[2 * n_pages * r:2 * n_pages * (r + 1)]
        o_ref[r] = _sattn_request(q_ref[r], kn_ref[r], vn_ref[r], pages[:n_pages], pages[n_pages:], pos0, ts)


def _sattn_request(qr, kn_new, vn_new, kp, vp, pos0, ts):
    n_pages = len(kp)
    nrow = qr.shape[0]
    ppb = MOBA_BLOCK // PAGE_SIZE
    n_past = n_pages // ppb
    blk_lane = lax.broadcasted_iota(jnp.int32, (BRANCH_W, LANES), 1)
    kmean_t = jnp.zeros((BRANCH_W, LANES), F32)
    for n in range(n_past):
        acc = kp[n * ppb][...]
        for r in range(1, ppb):
            acc = acc + kp[n * ppb + r][...]
        kmean_t = jnp.where(blk_lane == n, jnp.sum(acc, axis=1, keepdims=True) * (1.0 / MOBA_BLOCK), kmean_t)
    gate = jnp.dot(qr, kmean_t, precision=lax.Precision.HIGHEST, preferred_element_type=F32)
    bias = _top_blocks_bias(gate, n_past)
    qb = (qr * (A_DH ** -0.5)).astype(BF)
    scores = []
    for p in range(n_pages):
        s = jnp.dot(qb, kp[p][...].astype(BF), preferred_element_type=F32)
        scores.append(s + bias[:, p // ppb:p // ppb + 1])
    pad = jnp.zeros((LANES - kn_new.shape[0], BRANCH_W), F32)
    kn = jnp.concatenate([kn_new, pad], axis=0).astype(BF)
    vn = jnp.concatenate([vn_new, pad], axis=0).astype(BF)
    s_own = lax.dot_general(qb, kn, NT, preferred_element_type=F32)
    qi = lax.broadcasted_iota(jnp.int32, (nrow, LANES), 0) // A_HEADS
    kj = lax.broadcasted_iota(jnp.int32, (nrow, LANES), 1)
    own_blk = pos0 // MOBA_BLOCK
    ok = (kj < ts) & (own_blk * MOBA_BLOCK + kj <= pos0 + qi)
    s_own = jnp.where(ok, s_own, NEG_INF)
    m = jnp.max(s_own, axis=1, keepdims=True)
    for s in scores:
        m = jnp.maximum(m, jnp.max(s, axis=1, keepdims=True))
    p_own = jnp.exp(s_own - m)
    l = jnp.sum(p_own, axis=1, keepdims=True)
    out = jnp.dot(p_own.astype(BF), vn, preferred_element_type=F32)
    for p in range(n_pages):
        pr = jnp.exp(scores[p] - m)
        l = l + jnp.sum(pr, axis=1, keepdims=True)
        out = out + lax.dot_general(pr.astype(BF), vp[p][...].astype(BF), NT, preferred_element_type=F32)
    out = out / l
    hrow = lax.broadcasted_iota(jnp.int32, (nrow, BRANCH_W), 0) % A_HEADS
    hlane = lax.broadcasted_iota(jnp.int32, (nrow, BRANCH_W), 1) // A_DH
    out = jnp.where(hrow == hlane, out, 0.0)
    return jnp.sum(out.reshape(ts, A_HEADS, BRANCH_W), axis=1)


def _sample_attention(qrows, kn, vn, cache_k, cache_v, page_table, layer, pos0, ts):
    Bd, n_pages = page_table.shape
    nrow = qrows.shape[1]
    rps = REQS_PER_STEP
    assert Bd % rps == 0
    page = lambda r, p: pl.BlockSpec((None, None, BRANCH_W, PAGE_SIZE),
                                     lambda g, pt, r=r, p=p: (layer, pt[(g * rps + r) * n_pages + p], 0, 0))
    req = lambda rows: pl.BlockSpec((rps, rows, BRANCH_W), lambda g, pt: (g, 0, 0))
    pages = [page(r, p) for r in range(rps) for _ in range(2) for p in range(n_pages)]
    caches = [c for _ in range(rps) for c in ([cache_k] * n_pages + [cache_v] * n_pages)]
    grid_spec = pltpu.PrefetchScalarGridSpec(
        num_scalar_prefetch=1,
        grid=(Bd // rps,),
        in_specs=[req(nrow), req(kn.shape[1]), req(vn.shape[1])] + pages,
        out_specs=req(ts),
    )
    return pl.pallas_call(
        functools.partial(_sattn_body, n_pages=n_pages, pos0=pos0, ts=ts),
        grid_spec=grid_spec,
        out_shape=jax.ShapeDtypeStruct((Bd, ts, BRANCH_W), F32),
        compiler_params=_cp(("parallel",)),
        name="moba_sample",
    )(page_table.reshape(-1), qrows, kn, vn, *caches)


def _smix_body(uf_ref, db_ref, dc_ref, dh_ref, zb_ref, pw_ref, ps_ref, sw_ref, yb_ref, yd_ref, zn_ref, *, bd, ts, pos0):
    ps, sw = ps_ref[...], sw_ref[...]
    slab = lambda ref, r: ref[r * bd:(r + 1) * bd, :]
    z = [slab(zb_ref, r) for r in range(SCONV_W - 1)] + [slab(dc_ref, t) * slab(dh_ref, t) for t in range(ts)]
    for t in range(ts):
        u = slab(uf_ref, POOL_BUF + t)
        for g, w in enumerate(POOL_WINDOWS):
            sl = slice(g * POOL_GC, (g + 1) * POOL_GC)
            win = u[:, sl]
            for r in range(1, w):
                win = win + slab(uf_ref, POOL_BUF + t - r)[:, sl]
            pooled = win / float(min(w, pos0 + t + 1)) - u[:, sl]
            y = jnp.dot(pooled.astype(BF), pw_ref[g], preferred_element_type=F32) * ps[:, sl]
            yb_ref[t * bd:(t + 1) * bd, sl] = y.astype(yb_ref.dtype)
        conv = sw[0:1] * z[t] + sw[1:2] * z[t + 1] + sw[2:3] * z[t + 2]
        yd_ref[t * bd:(t + 1) * bd, :] = (slab(db_ref, t) * conv).astype(yd_ref.dtype)
    for r in range(SCONV_W - 1):
        zn_ref[r * bd:(r + 1) * bd, :] = z[ts + r]


def _sample_mixers(ufull, proj, zbuf, pool_w_bf, pool_scale, sconv_w, Bd, Ts, pos0):
    M = Ts * Bd
    whole = lambda a: pl.BlockSpec(a.shape, lambda i, n=a.ndim: (0,) * n)
    blk = lambda cb: pl.BlockSpec((M, BRANCH_W), lambda i, cb=cb: (0, cb))
    out = lambda r, dt: (pl.BlockSpec((r, BRANCH_W), lambda i: (0, 0)), jax.ShapeDtypeStruct((r, BRANCH_W), dt))
    outs = [out(M, BF), out(M, BF), out((SCONV_W - 1) * Bd, F32)]
    return pl.pallas_call(
        functools.partial(_smix_body, bd=Bd, ts=Ts, pos0=pos0),
        grid=(1,),
        in_specs=[whole(ufull), blk(CB_DB), blk(CB_DC), blk(CB_DH), whole(zbuf), whole(pool_w_bf), whole(pool_scale),
                  whole(sconv_w)],
        out_specs=[o[0] for o in outs],
        out_shape=[o[1] for o in outs],
        compiler_params=_cp(("arbitrary",)),
        name="pool_sconv_sample",
    )(ufull, proj, proj, proj, zbuf, pool_w_bf, pool_scale, sconv_w)


def _shgrn_body(cq_ref, cf_ref, ci_ref, cg_ref, s0_ref, lb_ref, ng_ref, yc_ref, so_ref, *, ts):
    lb, ng = lb_ref[...], ng_ref[...]
    for r in range(cq_ref.shape[0]):
        for h in range(C_HEADS):
            sl = slice(h * C_DK, (h + 1) * C_DK)
            o, st = _gla_chunk(cq_ref[r, :, sl], cf_ref[r, :, sl], ci_ref[r, :, sl], lb[:, sl], s0_ref[r, h].T, ts)
            so_ref[r, h] = st.T
            yc_ref[r, :, sl] = _hgrn_out(o, cg_ref[r, :, sl], ng[:, sl])


def _sample_hgrn(cq, cf, ci, cg, states, lb, ng, ts, layer, carried):
    Bd, rows, _ = cq.shape
    depth = states.shape[0]
    rps = math.gcd(Bd, 4)
    req = pl.BlockSpec((rps, rows, BRANCH_W), lambda b: (b, 0, 0))
    st_block = (rps, C_HEADS, C_DK, C_DK)
    st_in = pl.BlockSpec((None,) + st_block, lambda b: (layer, b, 0, 0, 0))
    st_out, st_shape = _layer_slot(layer, depth, states.shape[1:], st_block, lambda b: (b, 0, 0, 0))
    vec = pl.BlockSpec((1, BRANCH_W), lambda b: (0, 0))
    carried = [] if carried is None else [carried]
    n_in = 7
    body = functools.partial(_shgrn_body, ts=ts)
    if carried:
        body = lambda *refs, inner=body: inner(*refs[:n_in], *refs[n_in + 1:])
    return pl.pallas_call(
        body,
        grid=(Bd // rps,),
        in_specs=[req, req, req, req, st_in, vec, vec] + [pl.BlockSpec(memory_space=pl.ANY)] * len(carried),
        out_specs=[req, st_out],
        out_shape=[jax.ShapeDtypeStruct((Bd, rows, BRANCH_W), F32), st_shape],
        input_output_aliases={n_in: 1} if carried else {},
        compiler_params=_cp(("parallel",)),
        name="hgrn_sample",
    )(cq, cf, ci, cg, states, lb, ng, *carried)


def _prompt_layer(x, B, T, w, lb, tables, layer, depth, kv_carried):
    tm = min(T, 1024)
    proj = _matmul(x, w["w_mix"], tm, N_MIX // 2, "proj_prompt")
    gates = _matmul(x, w["w_gate"], tm, N_GATE // 2, "gate_prompt", gate=True)
    kt, vtf, kb, qt, vt, bias = _attn_prep(proj, tables, B, T, layer, depth, kv_carried)
    ya = _prompt_attention(qt, kb, vt, bias, B, T)
    yb, yd, ztail = _prompt_mixers(proj, w["pool_w"], w["pool_scale"], w["sconv_w"], B, T, min(T, 512))
    yc, s_new = _prompt_hgrn(proj, lb, w["hgrn_norm_g"], B, T, min(T, 512))
    h = _merge((ya, yb, yc, yd), gates, x, w["w_branch"], w["w_o"], w["ln1_g"], w["ln1_b"], 256)
    act, up_tail = _prompt_ffn_up(h, w["ffn_up"], w["ffn_conv"], B, T, 256)
    out = _prompt_ffn_down(act, w["ffn_down"], h, w["ln2_g"], w["ln2_b"], 512)
    proj3 = proj.reshape(B, T, N_MIX)
    new_pool = proj3[:, T - POOL_BUF:, CB_PU * BRANCH_W:(CB_PU + 1) * BRANCH_W]
    new_sconv = ztail[:, 8 - (SCONV_W - 1):]
    new_ffn = up_tail[:, 8 - (FFN_CONV_W - 1):]
    return out, (kt, vtf), new_pool, s_new, new_sconv, new_ffn


def _to_tm(a):
    return a.transpose(1, 0, 2).reshape(a.shape[1] * a.shape[0], a.shape[2])


def _from_tm(a, Bd):
    return a.reshape(a.shape[0] // Bd, Bd, a.shape[1]).transpose(1, 0, 2)


def _sample_layer(x, Bd, Ts, pos0, layer, w, lb, tables, cache_k, cache_v, page_table, pool_buf, hgrn_states, hgrn_carried,
                  sconv_buf, ffn_buf):
    M = Ts * Bd
    proj = _matmul(x, w["w_mix"], M, BRANCH_W, "proj_sample")
    gates = _matmul(x, w["w_gate"], M, BRANCH_W, "gate_sample", gate=True)
    qf, _, kf, _, _ = _rope(proj, tables, M)
    col = lambda cb: proj[:, cb * BRANCH_W:(cb + 1) * BRANCH_W]
    q_r, k_r, v_r = _from_tm(qf, Bd), _from_tm(kf, Bd), _from_tm(col(CB_AV), Bd)
    head_of_lane = jnp.arange(BRANCH_W) // A_DH
    hmask = (head_of_lane[None, :] == jnp.arange(A_HEADS)[:, None]).astype(F32)
    qrows = (q_r[:, :, None, :] * hmask[None, None]).reshape(Bd, Ts * A_HEADS, BRANCH_W)
    pad8 = lambda a: jnp.pad(a, ((0, 0), (0, 8 - Ts), (0, 0)))
    ya = _sample_attention(qrows, pad8(k_r), pad8(v_r), cache_k, cache_v, page_table, layer, pos0, Ts)
    ufull = jnp.concatenate([_to_tm(pool_buf), col(CB_PU)], axis=0)
    yb, yd, znew = _sample_mixers(ufull, proj, _to_tm(sconv_buf), w["pool_w"], w["pool_scale"], w["sconv_w"], Bd, Ts, pos0)
    req8 = lambda cb: pad8(_from_tm(col(cb), Bd))
    yc, s_new = _sample_hgrn(req8(CB_CQ), req8(CB_CF), req8(CB_CI), req8(CB_CG), hgrn_states, lb, w["hgrn_norm_g"], Ts,
                             layer, hgrn_carried)
    ys = (_to_tm(ya), yb, _to_tm(yc[:, :Ts]), yd)
    h = _merge(ys, gates, x, w["w_branch"], w["w_o"], w["ln1_g"], w["ln1_b"], min(M, 256))
    up = _matmul(h, w["ffn_up"], M, BRANCH_W, "ffn_up_sample")
    full = jnp.concatenate([_to_tm(ffn_buf), up], axis=0)
    out = _sample_ffn(full, w["ffn_conv"], w["ffn_down"], h, w["ln2_g"], w["ln2_b"], Bd, Ts)
    k = k_r.reshape(Bd, Ts, A_HEADS, A_DH)
    v = v_r.reshape(Bd, Ts, A_HEADS, A_DH)
    new_pool = _from_tm(ufull[Ts * Bd:], Bd)
    new_sconv = _from_tm(znew, Bd)
    new_ffn = _from_tm(full[Ts * Bd:], Bd)
    return out, k, v, new_pool, s_new, new_sconv, new_ffn


def _layer_weights(l, w_in, w_branch, w_o, pool_w, pool_scale, hgrn_norm_g, sconv_w, ln1_g, ln1_b, ffn_up, ffn_conv,
                   ffn_down, ln2_g, ln2_b):
    row = lambda a: a[l][None, :]
    return {
        "w_mix": w_in[l][:, :N_MIX].astype(BF), "w_gate": w_in[l][:, N_MIX:].astype(BF),
        "w_branch": w_branch[l].astype(BF), "w_o": w_o[l].astype(BF),
        "pool_w": pool_w[l].astype(BF), "pool_scale": row(pool_scale), "hgrn_norm_g": row(hgrn_norm_g),
        "sconv_w": sconv_w[l], "ln1_g": row(ln1_g), "ln1_b": row(ln1_b), "ffn_up": ffn_up[l].astype(BF),
        "ffn_conv": ffn_conv[l], "ffn_down": ffn_down[l].astype(BF), "ln2_g": row(ln2_g), "ln2_b": row(ln2_b),
    }


def kernel(x_prompt, x_sample, cache_k, cache_v, state_pool, state_hgrn, state_sconv, state_ffn, page_table, w_in,
           w_branch, w_o, pool_w, pool_scale, hgrn_lb_logits, hgrn_norm_g, sconv_w, ln1_g, ln1_b, ffn_up, ffn_conv,
           ffn_down, ln2_g, ln2_b):
    depth = w_in.shape[0]
    Bp, T, _ = x_prompt.shape
    Bd, Ts, _ = x_sample.shape
    n_pages = page_table.shape[1]
    pos0 = n_pages * PAGE_SIZE
    assert T % MOBA_BLOCK == 0 and pos0 % MOBA_BLOCK == 0 and Ts <= 8 and T >= HALO
    lb_all = _lower_bounds(hgrn_lb_logits.astype(F32))
    tab_p = _rope_tables(jnp.arange(T))
    tab_s = _rope_tables(pos0 + jnp.repeat(jnp.arange(Ts), Bd))
    ck = cache_k.transpose(0, 1, 3, 4, 2).reshape(depth, cache_k.shape[1], BRANCH_W, PAGE_SIZE)
    cv = cache_v.transpose(0, 1, 3, 4, 2).reshape(depth, cache_v.shape[1], BRANCH_W, PAGE_SIZE)
    hp = x_prompt.reshape(Bp * T, D_MODEL)
    hs = _to_tm(x_sample)
    outs_p, outs_s = [], []
    kv_p = hgrn_s = None
    for l in range(depth):
        w = _layer_weights(l, w_in, w_branch, w_o, pool_w, pool_scale, hgrn_norm_g, sconv_w, ln1_g, ln1_b, ffn_up,
                           ffn_conv, ffn_down, ln2_g, ln2_b)
        lb = lb_all[l][None, :]
        rp = _prompt_layer(hp, Bp, T, w, lb, tab_p, l, depth, kv_p)
        rs = _sample_layer(hs, Bd, Ts, pos0, l, w, lb, tab_s, ck, cv, page_table, state_pool[l], state_hgrn, hgrn_s,
                           state_sconv[l], state_ffn[l])
        hp, kv_p = rp[0], rp[1]
        hs, hgrn_s = rs[0], rs[4]
        outs_p.append(rp[2:])
        outs_s.append(rs[1:4] + rs[5:])
    stack = lambda outs: [jnp.stack([o[j] for o in outs]) for j in range(len(outs[0]))]
    tokens_first = lambda a: a.reshape(depth, Bp, A_HEADS, A_DH, T).transpose(0, 1, 4, 2, 3)
    pool_p, hgrn_p, sconv_p, ffn_p = stack(outs_p)
    k_s, v_s, pool_s, sconv_s, ffn_s = stack(outs_s)
    return (hp.reshape(Bp, T, D_MODEL), _from_tm(hs, Bd), tokens_first(kv_p[0]), tokens_first(kv_p[1]), pool_p, hgrn_p,
            sconv_p, ffn_p, k_s, v_s, pool_s, hgrn_s, sconv_s, ffn_s)
```

```python
import functools
import math

import jax
import jax.numpy as jnp
import numpy as np
from jax import lax
from jax.experimental import pallas as pl
from jax.experimental.pallas import tpu as pltpu

F32 = jnp.float32
BF = jnp.bfloat16
NEG_INF = float("-inf")
LOG2_E = math.log2(math.e)

D_MODEL = 1024
BRANCH_W = 512
N_BRANCH = 4
A_HEADS = 8
A_DH = 64
MOBA_BLOCK = 256
MOBA_TOPK = 3
ROPE_DIM = A_DH // 4
ROPE_THETA = 500000.0
POOL_WINDOWS = (2, 4, 8, 16)
POOL_GC = 128
POOL_BUF = 15
C_HEADS = 4
C_DK = 128
SCONV_W = 3
D_FF = 2816
FFN_CONV_W = 3
PAGE_SIZE = 128
DEPTH = 4
DN_ALPHA = (2 * DEPTH) ** 0.25
LN_EPS = 1e-5
RMS_EPS = 1e-6

CB_AQ, CB_AK, CB_AV, CB_PU, CB_CQ, CB_CF, CB_CI, CB_CG, CB_DB, CB_DC, CB_DH = range(11)
N_MIX = 11 * BRANCH_W
N_GATE = N_BRANCH * D_MODEL

LANES = 128
HALO = 16
SUM_ROWS = 16
ATTN_HEADS_PER_STEP = 4
VMEM_LIMIT = 56 * 1024 * 1024

NT = (((1,), (1,)), ((), ()))
TN = (((0,), (0,)), ((), ()))


def _cp(sem, vmem=VMEM_LIMIT):
    return pltpu.CompilerParams(dimension_semantics=sem, vmem_limit_bytes=vmem)


def _sigmoid(x):
    return 1.0 / (1.0 + jnp.exp(-x))


def _silu(x):
    return x * _sigmoid(x)


def _layer_norm(x, g, b):
    mu = jnp.mean(x, axis=-1, keepdims=True)
    xc = x - mu
    var = jnp.mean(xc * xc, axis=-1, keepdims=True)
    return xc * lax.rsqrt(var + LN_EPS) * g + b


def _lb_body(logit_ref, o_ref):
    x = logit_ref[...]
    m = jnp.max(x, axis=0, keepdims=True)
    e = jnp.exp(x - m)
    p = e / jnp.sum(e, axis=0, keepdims=True)
    acc = jnp.zeros_like(p[0:1])
    rows = []
    for l in range(x.shape[0]):
        acc = acc + p[l:l + 1]
        rows.append(acc - p[0:1])
    o_ref[...] = jnp.concatenate(rows, axis=0)


def _lower_bounds(logits):
    return pl.pallas_call(_lb_body, out_shape=jax.ShapeDtypeStruct(logits.shape, F32), name="hgrn_lb")(logits)


def _mm_body(x_ref, w_ref, o_ref, xb_ref, *, gate):
    @pl.when(pl.program_id(1) == 0)
    def _():
        xb_ref[...] = x_ref[...].astype(BF)

    y = jnp.dot(xb_ref[...], w_ref[...], preferred_element_type=F32)
    o_ref[...] = (_sigmoid(y) if gate else y).astype(o_ref.dtype)


def _matmul(x, w_bf, tm, tn, name, gate=False):
    M, K = x.shape
    N = w_bf.shape[1]
    assert M % tm == 0 and N % tn == 0
    return pl.pallas_call(
        functools.partial(_mm_body, gate=gate),
        grid=(M // tm, N // tn),
        in_specs=[pl.BlockSpec((tm, K), lambda i, j: (i, 0)),
                  pl.BlockSpec((K, tn), lambda i, j: (0, j))],
        out_specs=pl.BlockSpec((tm, tn), lambda i, j: (i, j)),
        out_shape=jax.ShapeDtypeStruct((M, N), BF if gate else F32),
        scratch_shapes=[pltpu.VMEM((tm, K), BF)],
        compiler_params=_cp(("parallel", "arbitrary")),
        name=name,
    )(x, w_bf)


def _rope_tables(pos):
    half = ROPE_DIM // 2
    inv = jnp.power(ROPE_THETA, -jnp.arange(0, ROPE_DIM, 2, dtype=F32) / ROPE_DIM)
    ang = pos.astype(F32)[:, None] * inv[None, :]
    cos, sin = jnp.cos(ang), jnp.sin(ang)
    n = pos.shape[0]
    ones = jnp.ones((n, A_DH - ROPE_DIM), F32)
    zeros = jnp.zeros((n, A_DH - ROPE_DIM), F32)
    zh = jnp.zeros((n, half), F32)
    c = jnp.concatenate([cos, cos, ones], axis=1)
    s_lo = jnp.concatenate([-sin, zh, zeros], axis=1)
    s_hi = jnp.concatenate([zh, sin, zeros], axis=1)
    tile = lambda a: jnp.tile(a, (1, A_HEADS))
    return tile(c), tile(s_lo), tile(s_hi)


def _rope_body(q_ref, k_ref, v_ref, c_ref, sl_ref, sh_ref, qf_ref, qb_ref, kf_ref, kb_ref, vb_ref):
    c, sl, sh = c_ref[...], sl_ref[...], sh_ref[...]
    half = ROPE_DIM // 2
    w = q_ref.shape[1]

    def rope(x):
        return x * c + pltpu.roll(x, w - half, 1) * sl + pltpu.roll(x, half, 1) * sh

    q = rope(q_ref[...])
    k = rope(k_ref[...])
    qf_ref[...] = q
    qb_ref[...] = (q * (A_DH ** -0.5)).astype(BF)
    kf_ref[...] = k
    kb_ref[...] = k.astype(BF)
    vb_ref[...] = v_ref[...].astype(BF)


def _rope(proj, tables, tm):
    M = proj.shape[0]
    ntab = tables[0].shape[0] // tm
    blk = lambda cb: pl.BlockSpec((tm, BRANCH_W), lambda i, cb=cb: (i, cb))
    tab = pl.BlockSpec((tm, BRANCH_W), lambda i: (i % ntab, 0))
    out = pl.BlockSpec((tm, BRANCH_W), lambda i: (i, 0))
    sds = lambda dt: jax.ShapeDtypeStruct((M, BRANCH_W), dt)
    return pl.pallas_call(
        _rope_body,
        grid=(M // tm,),
        in_specs=[blk(CB_AQ), blk(CB_AK), blk(CB_AV), tab, tab, tab],
        out_specs=[out] * 5,
        out_shape=[sds(F32), sds(BF), sds(F32), sds(BF), sds(BF)],
        compiler_params=_cp(("parallel",)),
        name="rope",
    )(proj, proj, proj, *tables)


def _top_blocks_bias(gate, n_past):
    col = lax.broadcasted_iota(jnp.int32, gate.shape, 1)
    g = jnp.where(col < n_past, gate, NEG_INF)
    bias = jnp.full(gate.shape, NEG_INF, F32)
    for _ in range(MOBA_TOPK):
        m = jnp.max(g, axis=1, keepdims=True)
        idx = jnp.min(jnp.where(g == m, col, LANES), axis=1, keepdims=True)
        pick = (col == idx) & (m > NEG_INF)
        bias = jnp.where(pick, 0.0, bias)
        g = jnp.where(pick, NEG_INF, g)
    return bias


def _split_bf16(x):
    hi = x.astype(BF)
    return hi, (x - hi.astype(F32)).astype(BF)


def _prep_body(qkv_ref, c_ref, sl_ref, sh_ref, kt_ref, vtf_ref, kb_ref, qt_ref, vt_ref, bias_ref, km_ref):
    i = pl.program_id(1)
    nbp = km_ref.shape[0]
    q_ref, k_ref, v_ref = (qkv_ref.at[:, n * BRANCH_W:(n + 1) * BRANCH_W] for n in range(3))

    @pl.when(i == 0)
    def _():
        km_ref[...] = jnp.zeros_like(km_ref)

    c, sl, sh = c_ref[...], sl_ref[...], sh_ref[...]
    half = ROPE_DIM // 2

    def rope(x):
        return x * c + pltpu.roll(x, BRANCH_W - half, 1) * sl + pltpu.roll(x, half, 1) * sh

    q = rope(q_ref[...])
    k = rope(k_ref[...])
    kt_ref[...] = k.T
    kb_ref[...] = k.astype(BF)
    qt_ref[...] = (q * (A_DH ** -0.5 * LOG2_E)).T.astype(BF)
    v_t = v_ref[...].T
    vtf_ref[...] = v_t
    vt_ref[...] = v_t.astype(BF)

    km = km_ref[...]
    lane = lax.broadcasted_iota(jnp.int32, km.shape, 1)
    stack = jnp.concatenate([jnp.where(lane // A_DH == h, km, 0.0) for h in range(A_HEADS)], axis=0)
    k_hi, k_lo = _split_bf16(stack)
    q_hi, q_lo = _split_bf16(q)
    dot = lambda a, b: lax.dot_general(a, b, NT, preferred_element_type=F32)
    gate = (dot(k_hi, q_hi) + dot(k_lo, q_hi) + dot(k_hi, q_lo)).reshape(A_HEADS, nbp, MOBA_BLOCK)
    blk = lax.broadcasted_iota(jnp.int32, gate.shape, 1)
    gate = jnp.where(blk < i, gate, NEG_INF)
    bias = jnp.full(gate.shape, NEG_INF, F32)
    for _ in range(MOBA_TOPK):
        m = jnp.max(gate, axis=1, keepdims=True)
        idx = jnp.min(jnp.where(gate == m, blk, nbp), axis=1, keepdims=True)
        pick = (blk == idx) & (m > NEG_INF)
        bias = jnp.where(pick, 0.0, bias)
        gate = jnp.where(pick, NEG_INF, gate)
    bias_ref[0] = bias.reshape(A_HEADS * nbp, MOBA_BLOCK)
    km_ref[pl.ds(i, 1), :] = jnp.sum(k, axis=0, keepdims=True) * (1.0 / MOBA_BLOCK)


def _layer_slot(layer, depth, shape, block, index_map):
    spec = pl.BlockSpec((None,) + block, lambda *a: (layer,) + tuple(index_map(*a)))
    return spec, jax.ShapeDtypeStruct((depth,) + shape, F32)


def _attn_prep(proj, tables, B, T, layer, depth, kv_carried):
    nq = T // MOBA_BLOCK
    nbp = -(-nq // 8) * 8
    tq = MOBA_BLOCK
    assert (CB_AQ, CB_AK, CB_AV) == tuple(range(CB_AQ, CB_AQ + 3)) and CB_AQ % 3 == 0
    qkv = pl.BlockSpec((tq, 3 * BRANCH_W), lambda b, i: (b * nq + i, CB_AQ // 3))
    tab = pl.BlockSpec((tq, BRANCH_W), lambda b, i: (i, 0))
    row = pl.BlockSpec((tq, BRANCH_W), lambda b, i: (b * nq + i, 0))
    tr = pl.BlockSpec((BRANCH_W, tq), lambda b, i: (b, i))
    slot, slot_shape = _layer_slot(layer, depth, (B * BRANCH_W, T), (BRANCH_W, tq), lambda b, i: (b, i))
    carried = [] if kv_carried is None else list(kv_carried)
    n_in = 4
    body = _prep_body if not carried else (lambda *refs: _prep_body(*refs[:n_in], *refs[n_in + len(carried):]))
    return pl.pallas_call(
        body,
        grid=(B, nq),
        in_specs=[qkv, tab, tab, tab] + [pl.BlockSpec(memory_space=pl.ANY)] * len(carried),
        out_specs=[slot, slot, row, tr, tr, pl.BlockSpec((1, A_HEADS * nbp, tq), lambda b, i: (b, 0, i))],
        out_shape=[slot_shape, slot_shape,
                   jax.ShapeDtypeStruct((B * T, BRANCH_W), BF),
                   jax.ShapeDtypeStruct((B * BRANCH_W, T), BF), jax.ShapeDtypeStruct((B * BRANCH_W, T), BF),
                   jax.ShapeDtypeStruct((B, A_HEADS * nbp, T), F32)],
        input_output_aliases={n_in + n: n for n in range(len(carried))},
        scratch_shapes=[pltpu.VMEM((nbp, BRANCH_W), F32)],
        compiler_params=_cp(("parallel", "arbitrary")),
        name="attn_prep",
    )(proj, *tables, *carried)


def _attn_body(qt_ref, k_ref, vt_ref, bias_ref, o_ref, sa_ref, sb_ref, pp_ref):
    i = pl.program_id(2)
    tq = MOBA_BLOCK
    npair = qt_ref.shape[0] // A_DH
    nbp = bias_ref.shape[1] // npair
    qt = qt_ref[...]
    rowi = lax.broadcasted_iota(jnp.int32, qt.shape, 0)
    qts = [jnp.where((rowi >= e * A_DH) & (rowi < (e + 1) * A_DH), qt, jnp.zeros_like(qt)) for e in range(npair)]
    block = lambda j: pl.ds(j * tq if isinstance(j, int) else pl.multiple_of(j * tq, tq), tq)
    ones = jnp.ones((SUM_ROWS, tq), BF)

    def scores(s_ref, j):
        kj = k_ref[block(j), :]
        for e in range(npair):
            s_ref[e] = jnp.dot(kj, qts[e], preferred_element_type=F32)

    def pending_product(e, jp):
        vt = jnp.concatenate([vt_ref[e * A_DH:(e + 1) * A_DH, block(jp)], ones], axis=0)
        return jnp.dot(vt, pp_ref[e], preferred_element_type=F32)

    def settle(e, head, jp):
        m, l, acc, pa = head
        pv = pending_product(e, jp)
        return m, pa * l + pv[A_DH:A_DH + 1], pa * acc + pv[:A_DH]

    def consume(s_ref, j, carry, own):
        heads, jp = carry
        new = []
        for e in range(npair):
            m, l, acc = settle(e, heads[e], jp)
            s = s_ref[e]
            if own:
                causal = (lax.broadcasted_iota(jnp.int32, (tq, tq), 0) <= lax.broadcasted_iota(jnp.int32, (tq, tq), 1))
                s = jnp.where(causal, s, NEG_INF)
                mn = jnp.maximum(m, jnp.max(s, axis=0, keepdims=True))
                shift = mn
            else:
                b = bias_ref[0, pl.ds(e * nbp + j, 1), :]
                mn = jnp.maximum(m, jnp.max(s, axis=0, keepdims=True) + b)
                shift = mn - b
            pp_ref[e] = jnp.exp2(s - shift).astype(BF)
            new.append((mn, l, acc, jnp.exp2(m - mn)))
        return tuple(new), j

    def two_blocks(p, carry):
        scores(sb_ref, 2 * p + 1)
        carry = consume(sa_ref, 2 * p, carry, own=False)
        scores(sa_ref, 2 * p + 2)
        return consume(sb_ref, 2 * p + 1, carry, own=False)

    def odd_tail(carry):
        scores(sb_ref, i)
        carry = consume(sa_ref, i - 1, carry, own=False)
        return consume(sb_ref, i, carry, own=True)

    def even_tail(carry):
        return consume(sa_ref, i, carry, own=True)

    pp_ref[...] = jnp.zeros_like(pp_ref)
    init = tuple((jnp.full((1, tq), -1e30, F32), jnp.zeros((1, tq), F32), jnp.zeros((A_DH, tq), F32),
                  jnp.ones((1, tq), F32)) for _ in range(npair))
    scores(sa_ref, 0)
    state = lax.fori_loop(0, i // 2, two_blocks, (init, jnp.int32(0)))
    heads, jp = lax.cond(i % 2 == 1, odd_tail, even_tail, state)
    outs = []
    for e in range(npair):
        _, l, acc = settle(e, heads[e], jp)
        outs.append(acc / l)
    o_ref[...] = jnp.concatenate(outs, axis=0).T.astype(o_ref.dtype)


def _prompt_attention(qt, kb, vt, bias, B, T):
    nq = T // MOBA_BLOCK
    npair = ATTN_HEADS_PER_STEP
    width = npair * A_DH
    ngrp = BRANCH_W // width
    nbp = bias.shape[1] // A_HEADS
    return pl.pallas_call(
        _attn_body,
        grid=(B, ngrp, nq),
        in_specs=[pl.BlockSpec((width, MOBA_BLOCK), lambda b, hp, i: (b * ngrp + hp, i)),
                  pl.BlockSpec((T, width), lambda b, hp, i: (b, hp)),
                  pl.BlockSpec((width, T), lambda b, hp, i: (b * ngrp + hp, 0)),
                  pl.BlockSpec((1, npair * nbp, MOBA_BLOCK), lambda b, hp, i: (b, hp, i))],
        out_specs=pl.BlockSpec((MOBA_BLOCK, width), lambda b, hp, i: (b * nq + i, hp)),
        out_shape=jax.ShapeDtypeStruct((B * T, BRANCH_W), BF),
        scratch_shapes=[pltpu.VMEM((npair, MOBA_BLOCK, MOBA_BLOCK), F32)] * 2
        + [pltpu.VMEM((npair, MOBA_BLOCK, MOBA_BLOCK), BF)],
        compiler_params=_cp(("parallel", "parallel", "arbitrary")),
        name="moba_prompt",
    )(qt, kb, vt, bias)


def _pool_group(ext_ref, u, g, w, t_first, pw, ps):
    tm = u.shape[0]
    sl = slice(g * POOL_GC, (g + 1) * POOL_GC)
    win = ext_ref[HALO:HALO + tm, sl]
    for r in range(1, w):
        win = win + ext_ref[HALO - r:HALO - r + tm, sl]
    t = t_first + lax.broadcasted_iota(jnp.int32, (tm, POOL_GC), 0)
    cnt = jnp.minimum(w, t + 1).astype(F32)
    pooled = win / cnt - u[:, sl]
    return jnp.dot(pooled.astype(BF), pw, preferred_element_type=F32) * ps[:, sl]


def _mix_body(u_ref, uh_ref, db_ref, dc_ref, dh_ref, dch_ref, dhh_ref, pw_ref, ps_ref, sw_ref,
              yb_ref, yd_ref, zt_ref, ext_ref, zext_ref):
    i = pl.program_id(1)
    tm = u_ref.shape[0]
    first = i == 0
    u = u_ref[...]
    ext_ref[0:HALO, :] = jnp.where(first, 0.0, uh_ref[...])
    ext_ref[HALO:, :] = u
    ps = ps_ref[...]
    for g, w in enumerate(POOL_WINDOWS):
        y = _pool_group(ext_ref, u, g, w, i * tm, pw_ref[g], ps)
        yb_ref[:, g * POOL_GC:(g + 1) * POOL_GC] = y.astype(yb_ref.dtype)
    z = dc_ref[...] * dh_ref[...]
    zext_ref[0:8, :] = jnp.where(first, 0.0, dch_ref[...] * dhh_ref[...])
    zext_ref[8:, :] = z
    sw = sw_ref[...]
    conv = sw[0:1] * zext_ref[6:6 + tm, :] + sw[1:2] * zext_ref[7:7 + tm, :] + sw[2:3] * z
    yd_ref[...] = (db_ref[...] * conv).astype(yd_ref.dtype)
    zt_ref[0] = z[tm - 8:tm]


def _prompt_mixers(proj, pool_w_bf, pool_scale, sconv_w, B, T, tm):
    nt = T // tm
    row = lambda b, i: b * nt + i
    blk = lambda cb: pl.BlockSpec((tm, BRANCH_W), lambda b, i, cb=cb: (row(b, i), cb))

    def halo(cb, h):
        return pl.BlockSpec((h, BRANCH_W), lambda b, i, cb=cb, h=h: (jnp.maximum(row(b, i) * (tm // h) - 1, 0), cb))

    const = lambda shape: pl.BlockSpec(shape, lambda b, i, n=len(shape): (0,) * n)
    return pl.pallas_call(
        _mix_body,
        grid=(B, nt),
        in_specs=[blk(CB_PU), halo(CB_PU, HALO), blk(CB_DB), blk(CB_DC), blk(CB_DH), halo(CB_DC, 8), halo(CB_DH, 8),
                  const((len(POOL_WINDOWS), POOL_GC, POOL_GC)), const((1, BRANCH_W)), const((SCONV_W, BRANCH_W))],
        out_specs=[pl.BlockSpec((tm, BRANCH_W), lambda b, i: (row(b, i), 0)),
                   pl.BlockSpec((tm, BRANCH_W), lambda b, i: (row(b, i), 0)),
                   pl.BlockSpec((1, 8, BRANCH_W), lambda b, i: (b, 0, 0))],
        out_shape=[jax.ShapeDtypeStruct((B * T, BRANCH_W), BF), jax.ShapeDtypeStruct((B * T, BRANCH_W), BF),
                   jax.ShapeDtypeStruct((B, 8, BRANCH_W), F32)],
        scratch_shapes=[pltpu.VMEM((tm + HALO, BRANCH_W), F32), pltpu.VMEM((tm + 8, BRANCH_W), F32)],
        compiler_params=_cp(("parallel", "arbitrary")),
        name="pool_sconv_prompt",
    )(proj, proj, proj, proj, proj, proj, proj, pool_w_bf, pool_scale, sconv_w)


def _gla_chunk(cq, cf, ci, lb, st, n_valid):
    rows = cq.shape[0]
    f = lb + (1.0 - lb) * _sigmoid(cf)
    g = jnp.log(f)
    r_i = lax.broadcasted_iota(jnp.int32, (rows, rows), 0)
    c_i = lax.broadcasted_iota(jnp.int32, (rows, rows), 1)
    tri = jnp.where((r_i >= c_i) & (c_i < n_valid), 1.0, 0.0).astype(F32)
    b = jnp.dot(tri, g, precision=lax.Precision.HIGHEST, preferred_element_type=F32)
    qs = _silu(cq)
    kk = 1.0 - f
    t_i = lax.broadcasted_iota(jnp.int32, (rows, 1), 0)
    o = jnp.zeros((rows, C_DK), F32)
    for s in range(n_valid):
        e = jnp.exp(jnp.minimum(b - b[s:s + 1], 0.0))
        a = jnp.sum(qs * kk[s:s + 1] * e, axis=1, keepdims=True)
        o = o + jnp.where(t_i >= s, a, 0.0) * ci[s:s + 1]
    o = o + lax.dot_general((qs * jnp.exp(b)).astype(BF), st.astype(BF), NT, preferred_element_type=F32)
    bl = b[n_valid - 1:n_valid]
    ks = jnp.where(t_i < n_valid, kk * jnp.exp(jnp.minimum(bl - b, 0.0)), 0.0)
    upd = lax.dot_general(ci.astype(BF), ks.astype(BF), TN, preferred_element_type=F32)
    return o, st * jnp.exp(bl) + upd


def _hgrn_out(o, cg, ng):
    o = o * lax.rsqrt(jnp.mean(o * o, axis=1, keepdims=True) + RMS_EPS)
    return o * ng * _silu(cg)


GROUP = 128
GROUP_HALVES = tuple(GROUP >> (s + 1) for s in range(int(math.log2(GROUP))))


def _decay_sum_matrix():
    t = np.arange(GROUP)[:, None]
    u = np.arange(GROUP)[None, :]
    blocks = [u <= t, u > t]
    for h in GROUP_HALVES:
        r = (t // (2 * h)) * 2 * h + h - 1
        second = (t // h) % 2 == 1
        blocks.append(np.where(second, (u > r) & (u <= t), (u > t) & (u <= r)))
    d = np.concatenate(blocks, axis=0).astype(np.float32)
    return jnp.asarray(np.concatenate([d, d], axis=1), dtype=BF)


def _gla_decays(cf, lb, dmat):
    f = lb + (1.0 - lb) * _sigmoid(cf)
    g_hi, g_lo = _split_bf16(jnp.log(f))
    return f, jnp.dot(dmat, jnp.concatenate([g_hi, g_lo], axis=0), preferred_element_type=F32)


def _gla_group(cq, f, x, ci, st):
    e = jnp.exp(x)
    blk = lambda n: e[n * GROUP:(n + 1) * GROUP]
    qs = _silu(cq)
    kk = 1.0 - f
    v = ci.astype(BF)
    t_i = lax.broadcasted_iota(jnp.int32, (GROUP, GROUP), 0)
    s_i = lax.broadcasted_iota(jnp.int32, (GROUP, GROUP), 1)
    row = lax.broadcasted_iota(jnp.int32, (GROUP, C_DK), 0)
    dot_nt = lambda a, b: lax.dot_general(a.astype(BF), b.astype(BF), NT, preferred_element_type=F32)
    attn = jnp.where(t_i == s_i, dot_nt(qs, kk), 0.0)
    for s, h in enumerate(GROUP_HALVES):
        second = (row // h) % 2 == 1
        scaled_q = jnp.where(second, qs * blk(2 + s), 0.0)
        scaled_k = jnp.where(second, 0.0, kk * blk(2 + s))
        part = dot_nt(scaled_q, scaled_k)
        attn = attn + (part if 2 * h == GROUP else jnp.where(t_i // (2 * h) == s_i // (2 * h), part, 0.0))
    o = jnp.dot(attn.astype(BF), v, preferred_element_type=F32) + dot_nt(qs * blk(0), st)
    upd = lax.dot_general(v, (kk * blk(1)).astype(BF), TN, preferred_element_type=F32)
    return o, st * e[GROUP - 1:GROUP] + upd


def _hgrn_body(c_ref, lb_ref, ng_ref, dm_ref, yc_ref, so_ref, st_ref):
    it = pl.program_id(1)
    tc = c_ref.shape[0]
    cq_ref, cf_ref, ci_ref, cg_ref = (c_ref.at[:, n * BRANCH_W:(n + 1) * BRANCH_W] for n in range(4))

    @pl.when(it == 0)
    def _():
        st_ref[...] = jnp.zeros_like(st_ref)

    lb, ng = lb_ref[...], ng_ref[...]

    def group(gi, carry):
        rows = pl.ds(pl.multiple_of(gi * GROUP, GROUP), GROUP)
        head = lambda h: slice(h * C_DK, (h + 1) * C_DK)
        f, x = _gla_decays(cf_ref[rows, :], lb, dm_ref[...])
        for h in range(C_HEADS):
            sl = head(h)
            o, st = _gla_group(cq_ref[rows, sl], f[:, sl], x[:, sl], ci_ref[rows, sl], st_ref[h])
            st_ref[h] = st
            yc_ref[rows, sl] = _hgrn_out(o, cg_ref[rows, sl], ng[:, sl]).astype(yc_ref.dtype)
        return carry

    lax.fori_loop(0, tc // GROUP, group, 0)

    @pl.when(it == pl.num_programs(1) - 1)
    def _():
        for h in range(C_HEADS):
            so_ref[0, h] = st_ref[h].T


def _prompt_hgrn(proj, lb, ng, B, T, tc):
    nt = T // tc
    dmat = _decay_sum_matrix()
    assert (CB_CQ, CB_CF, CB_CI, CB_CG) == tuple(range(CB_CQ, CB_CQ + 4)) and CB_CQ % 4 == 0
    vec = pl.BlockSpec((1, BRANCH_W), lambda b, i: (0, 0))
    return pl.pallas_call(
        _hgrn_body,
        grid=(B, nt),
        in_specs=[pl.BlockSpec((tc, 4 * BRANCH_W), lambda b, i: (b * nt + i, CB_CQ // 4)), vec, vec,
                  pl.BlockSpec(dmat.shape, lambda b, i: (0, 0))],
        out_specs=[pl.BlockSpec((tc, BRANCH_W), lambda b, i: (b * nt + i, 0)),
                   pl.BlockSpec((1, C_HEADS, C_DK, C_DK), lambda b, i: (b, 0, 0, 0))],
        out_shape=[jax.ShapeDtypeStruct((B * T, BRANCH_W), BF), jax.ShapeDtypeStruct((B, C_HEADS, C_DK, C_DK), F32)],
        scratch_shapes=[pltpu.VMEM((C_HEADS, C_DK, C_DK), F32)],
        compiler_params=_cp(("parallel", "arbitrary")),
        name="hgrn_prompt",
    )(proj, lb, ng, dmat)


def _merge_body(ya_ref, yb_ref, yc_ref, yd_ref, g_ref, x_ref, wb_ref, wo_ref, lg_ref, lbias_ref, h_ref):
    mixed = jnp.zeros(x_ref.shape, F32)
    for n, y_ref in enumerate((ya_ref, yb_ref, yc_ref, yd_ref)):
        br = jnp.dot(y_ref[...].astype(BF), wb_ref[n], preferred_element_type=F32)
        mixed = mixed + g_ref[:, n * D_MODEL:(n + 1) * D_MODEL].astype(F32) * br
    pre = DN_ALPHA * x_ref[...] + jnp.dot(mixed.astype(BF), wo_ref[...], preferred_element_type=F32)
    h_ref[...] = _layer_norm(pre, lg_ref[...], lbias_ref[...])


def _merge(ys, gates, x, wb_bf, wo_bf, ln_g, ln_b, tm):
    M = x.shape[0]
    yspec = pl.BlockSpec((tm, BRANCH_W), lambda i: (i, 0))
    gspec = pl.BlockSpec((tm, N_GATE), lambda i: (i, 0))
    xspec = pl.BlockSpec((tm, D_MODEL), lambda i: (i, 0))
    const = lambda shape: pl.BlockSpec(shape, lambda i, n=len(shape): (0,) * n)
    return pl.pallas_call(
        _merge_body,
        grid=(M // tm,),
        in_specs=[yspec] * N_BRANCH + [gspec, xspec, const((N_BRANCH, BRANCH_W, D_MODEL)), const((D_MODEL, D_MODEL)),
                                       const((1, D_MODEL)), const((1, D_MODEL))],
        out_specs=xspec,
        out_shape=jax.ShapeDtypeStruct((M, D_MODEL), F32),
        compiler_params=_cp(("parallel",)),
        name="merge_ln1",
    )(*ys, gates, x, wb_bf, wo_bf, ln_g, ln_b)


def _ffn_finish(gate, val, wd_ref, h_ref, lg_ref, lb_ref, o_ref):
    a = (_silu(gate) * val).astype(BF)
    pre = DN_ALPHA * h_ref[...] + jnp.dot(a, wd_ref[...], preferred_element_type=F32)
    o_ref[...] = _layer_norm(pre, lg_ref[...], lb_ref[...])


FFN_COLS = 256


def _ffn_up_body(x_ref, w_ref, cw_ref, a_ref, tail_ref, halo_ref, *, tiles_per_seq):
    i = pl.program_id(0)
    tm = x_ref.shape[0]

    @pl.when(i == 0)
    def _():
        halo_ref[...] = jnp.zeros_like(halo_ref)

    xb = x_ref[...].astype(BF)
    seq_start = i % tiles_per_seq == 0

    def conv(cols):
        u = jnp.dot(xb, w_ref[:, cols], preferred_element_type=F32)
        prev = jnp.where(seq_start, 0.0, halo_ref[:, cols])
        halo_ref[:, cols] = u[tm - 8:tm]
        tail_ref[0, :, cols] = u[tm - 8:tm]
        cw = cw_ref[:, cols]
        back1 = jnp.concatenate([prev[7:8], u[:tm - 1]], axis=0)
        back2 = jnp.concatenate([prev[6:8], u[:tm - 2]], axis=0)
        return cw[0:1] * back2 + cw[1:2] * back1 + cw[2:3] * u

    for c in range(D_FF // FFN_COLS):
        gate = conv(slice(c * FFN_COLS, (c + 1) * FFN_COLS))
        val = conv(slice(D_FF + c * FFN_COLS, D_FF + (c + 1) * FFN_COLS))
        a_ref[:, c * FFN_COLS:(c + 1) * FFN_COLS] = (_silu(gate) * val).astype(a_ref.dtype)


def _prompt_ffn_up(h, wu_bf, ffn_conv, B, T, tm):
    M, K = h.shape
    nt = T // tm
    assert D_FF % FFN_COLS == 0
    const = lambda shape: pl.BlockSpec(shape, lambda i, n=len(shape): (0,) * n)
    return pl.pallas_call(
        functools.partial(_ffn_up_body, tiles_per_seq=nt),
        grid=(M // tm,),
        in_specs=[pl.BlockSpec((tm, K), lambda i: (i, 0)), const((K, 2 * D_FF)), const((FFN_CONV_W, 2 * D_FF))],
        out_specs=[pl.BlockSpec((tm, D_FF), lambda i: (i, 0)),
                   pl.BlockSpec((1, 8, 2 * D_FF), lambda i: (i // nt, 0, 0))],
        out_shape=[jax.ShapeDtypeStruct((M, D_FF), BF), jax.ShapeDtypeStruct((B, 8, 2 * D_FF), F32)],
        scratch_shapes=[pltpu.VMEM((8, 2 * D_FF), F32)],
        compiler_params=_cp(("arbitrary",)),
        name="ffn_up_act_prompt",
    )(h, wu_bf, ffn_conv)


def _ffn_down_body(a_ref, wd_ref, h_ref, lg_ref, lb_ref, o_ref):
    pre = DN_ALPHA * h_ref[...] + jnp.dot(a_ref[...], wd_ref[...], preferred_element_type=F32)
    o_ref[...] = _layer_norm(pre, lg_ref[...], lb_ref[...])


def _prompt_ffn_down(a, wd_bf, h, ln_g, ln_b, tm):
    M = h.shape[0]
    const = lambda shape: pl.BlockSpec(shape, lambda i, n=len(shape): (0,) * n)
    xspec = pl.BlockSpec((tm, D_MODEL), lambda i: (i, 0))
    return pl.pallas_call(
        _ffn_down_body,
        grid=(M // tm,),
        in_specs=[pl.BlockSpec((tm, D_FF), lambda i: (i, 0)), const((D_FF, D_MODEL)), xspec,
                  const((1, D_MODEL)), const((1, D_MODEL))],
        out_specs=xspec,
        out_shape=jax.ShapeDtypeStruct((M, D_MODEL), F32),
        compiler_params=_cp(("parallel",)),
        name="ffn_down_prompt",
    )(a, wd_bf, h, ln_g, ln_b)


def _ffn_sample_body(g0_ref, g1_ref, g2_ref, v0_ref, v1_ref, v2_ref, cwg_ref, cwv_ref, wd_ref, h_ref, lg_ref, lb_ref, o_ref):
    cwg, cwv = cwg_ref[...], cwv_ref[...]
    gate = cwg[0:1] * g0_ref[...] + cwg[1:2] * g1_ref[...] + cwg[2:3] * g2_ref[...]
    val = cwv[0:1] * v0_ref[...] + cwv[1:2] * v1_ref[...] + cwv[2:3] * v2_ref[...]
    _ffn_finish(gate, val, wd_ref, h_ref, lg_ref, lb_ref, o_ref)


def _sample_ffn(full, ffn_conv, wd_bf, h, ln_g, ln_b, Bd, Ts):
    blk = lambda c, r: pl.BlockSpec((Bd, D_FF), lambda t, c=c, r=r: (t + r, c))
    cw = lambda c: pl.BlockSpec((FFN_CONV_W, D_FF), lambda t, c=c: (0, c))
    const = lambda shape: pl.BlockSpec(shape, lambda t, n=len(shape): (0,) * n)
    xspec = pl.BlockSpec((Bd, D_MODEL), lambda t: (t, 0))
    return pl.pallas_call(
        _ffn_sample_body,
        grid=(Ts,),
        in_specs=[blk(0, 0), blk(0, 1), blk(0, 2), blk(1, 0), blk(1, 1), blk(1, 2), cw(0), cw(1),
                  const((D_FF, D_MODEL)), xspec, const((1, D_MODEL)), const((1, D_MODEL))],
        out_specs=xspec,
        out_shape=jax.ShapeDtypeStruct((Ts * Bd, D_MODEL), F32),
        compiler_params=_cp(("parallel",)),
        name="ffn_tail_sample",
    )(full, full, full, full, full, full, ffn_conv, ffn_conv, wd_bf, h, ln_g, ln_b)


REQS_PER_STEP = 2


def _sattn_body(pt_ref, q_ref, kn_ref, vn_ref, *rest, n_pages, pos0, ts):
    del pt_ref
    o_ref = rest[2 * n_pages * REQS_PER_STEP]
    for r in range(REQS_PER_STEP):
        pages = rest[2 * n_pages * r:2 * n_pages * (r + 1)]
        o_ref[r] = _sattn_request(q_ref[r], kn_ref[r], vn_ref[r], pages[:n_pages], pages[n_pages:], pos0, ts)


def _sattn_request(qr, kn_new, vn_new, kp, vp, pos0, ts):
    n_pages = len(kp)
    nrow = qr.shape[0]
    ppb = MOBA_BLOCK // PAGE_SIZE
    n_past = n_pages // ppb
    blk_lane = lax.broadcasted_iota(jnp.int32, (BRANCH_W, LANES), 1)
    kmean_t = jnp.zeros((BRANCH_W, LANES), F32)
    for n in range(n_past):
        acc = kp[n * ppb][...]
        for r in range(1, ppb):
            acc = acc + kp[n * ppb + r][...]
        kmean_t = jnp.where(blk_lane == n, jnp.sum(acc, axis=1, keepdims=True) * (1.0 / MOBA_BLOCK), kmean_t)
    gate = jnp.dot(qr, kmean_t, precision=lax.Precision.HIGHEST, preferred_element_type=F32)
    bias = _top_blocks_bias(gate, n_past)
    qb = (qr * (A_DH ** -0.5)).astype(BF)
    scores = []
    for p in range(n_pages):
        s = jnp.dot(qb, kp[p][...].astype(BF), preferred_element_type=F32)
        scores.append(s + bias[:, p // ppb:p // ppb + 1])
    pad = jnp.zeros((LANES - kn_new.shape[0], BRANCH_W), F32)
    kn = jnp.concatenate([kn_new, pad], axis=0).astype(BF)
    vn = jnp.concatenate([vn_new, pad], axis=0).astype(BF)
    s_own = lax.dot_general(qb, kn, NT, preferred_element_type=F32)
    qi = lax.broadcasted_iota(jnp.int32, (nrow, LANES), 0) // A_HEADS
    kj = lax.broadcasted_iota(jnp.int32, (nrow, LANES), 1)
    own_blk = pos0 // MOBA_BLOCK
    ok = (kj < ts) & (own_blk * MOBA_BLOCK + kj <= pos0 + qi)
    s_own = jnp.where(ok, s_own, NEG_INF)
    m = jnp.max(s_own, axis=1, keepdims=True)
    for s in scores:
        m = jnp.maximum(m, jnp.max(s, axis=1, keepdims=True))
    p_own = jnp.exp(s_own - m)
    l = jnp.sum(p_own, axis=1, keepdims=True)
    out = jnp.dot(p_own.astype(BF), vn, preferred_element_type=F32)
    for p in range(n_pages):
        pr = jnp.exp(scores[p] - m)
        l = l + jnp.sum(pr, axis=1, keepdims=True)
        out = out + lax.dot_general(pr.astype(BF), vp[p][...].astype(BF), NT, preferred_element_type=F32)
    out = out / l
    hrow = lax.broadcasted_iota(jnp.int32, (nrow, BRANCH_W), 0) % A_HEADS
    hlane = lax.broadcasted_iota(jnp.int32, (nrow, BRANCH_W), 1) // A_DH
    out = jnp.where(hrow == hlane, out, 0.0)
    return jnp.sum(out.reshape(ts, A_HEADS, BRANCH_W), axis=1)


def _sample_attention(qrows, kn, vn, cache_k, cache_v, page_table, layer, pos0, ts):
    Bd, n_pages = page_table.shape
    nrow = qrows.shape[1]
    rps = REQS_PER_STEP
    assert Bd % rps == 0
    page = lambda r, p: pl.BlockSpec((None, None, BRANCH_W, PAGE_SIZE),
                                     lambda g, pt, r=r, p=p: (layer, pt[(g * rps + r) * n_pages + p], 0, 0))
    req = lambda rows: pl.BlockSpec((rps, rows, BRANCH_W), lambda g, pt: (g, 0, 0))
    pages = [page(r, p) for r in range(rps) for _ in range(2) for p in range(n_pages)]
    caches = [c for _ in range(rps) for c in ([cache_k] * n_pages + [cache_v] * n_pages)]
    grid_spec = pltpu.PrefetchScalarGridSpec(
        num_scalar_prefetch=1,
        grid=(Bd // rps,),
        in_specs=[req(nrow), req(kn.shape[1]), req(vn.shape[1])] + pages,
        out_specs=req(ts),
    )
    return pl.pallas_call(
        functools.partial(_sattn_body, n_pages=n_pages, pos0=pos0, ts=ts),
        grid_spec=grid_spec,
        out_shape=jax.ShapeDtypeStruct((Bd, ts, BRANCH_W), F32),
        compiler_params=_cp(("parallel",)),
        name="moba_sample",
    )(page_table.reshape(-1), qrows, kn, vn, *caches)


def _smix_body(uf_ref, db_ref, dc_ref, dh_ref, zb_ref, pw_ref, ps_ref, sw_ref, yb_ref, yd_ref, zn_ref, *, bd, ts, pos0):
    ps, sw = ps_ref[...], sw_ref[...]
    slab = lambda ref, r: ref[r * bd:(r + 1) * bd, :]
    z = [slab(zb_ref, r) for r in range(SCONV_W - 1)] + [slab(dc_ref, t) * slab(dh_ref, t) for t in range(ts)]
    for t in range(ts):
        u = slab(uf_ref, POOL_BUF + t)
        for g, w in enumerate(POOL_WINDOWS):
            sl = slice(g * POOL_GC, (g + 1) * POOL_GC)
            win = u[:, sl]
            for r in range(1, w):
                win = win + slab(uf_ref, POOL_BUF + t - r)[:, sl]
            pooled = win / float(min(w, pos0 + t + 1)) - u[:, sl]
            y = jnp.dot(pooled.astype(BF), pw_ref[g], preferred_element_type=F32) * ps[:, sl]
            yb_ref[t * bd:(t + 1) * bd, sl] = y.astype(yb_ref.dtype)
        conv = sw[0:1] * z[t] + sw[1:2] * z[t + 1] + sw[2:3] * z[t + 2]
        yd_ref[t * bd:(t + 1) * bd, :] = (slab(db_ref, t) * conv).astype(yd_ref.dtype)
    for r in range(SCONV_W - 1):
        zn_ref[r * bd:(r + 1) * bd, :] = z[ts + r]


def _sample_mixers(ufull, proj, zbuf, pool_w_bf, pool_scale, sconv_w, Bd, Ts, pos0):
    M = Ts * Bd
    whole = lambda a: pl.BlockSpec(a.shape, lambda i, n=a.ndim: (0,) * n)
    blk = lambda cb: pl.BlockSpec((M, BRANCH_W), lambda i, cb=cb: (0, cb))
    out = lambda r, dt: (pl.BlockSpec((r, BRANCH_W), lambda i: (0, 0)), jax.ShapeDtypeStruct((r, BRANCH_W), dt))
    outs = [out(M, BF), out(M, BF), out((SCONV_W - 1) * Bd, F32)]
    return pl.pallas_call(
        functools.partial(_smix_body, bd=Bd, ts=Ts, pos0=pos0),
        grid=(1,),
        in_specs=[whole(ufull), blk(CB_DB), blk(CB_DC), blk(CB_DH), whole(zbuf), whole(pool_w_bf), whole(pool_scale),
                  whole(sconv_w)],
        out_specs=[o[0] for o in outs],
        out_shape=[o[1] for o in outs],
        compiler_params=_cp(("arbitrary",)),
        name="pool_sconv_sample",
    )(ufull, proj, proj, proj, zbuf, pool_w_bf, pool_scale, sconv_w)


def _shgrn_body(cq_ref, cf_ref, ci_ref, cg_ref, s0_ref, lb_ref, ng_ref, yc_ref, so_ref, *, ts):
    lb, ng = lb_ref[...], ng_ref[...]
    for r in range(cq_ref.shape[0]):
        for h in range(C_HEADS):
            sl = slice(h * C_DK, (h + 1) * C_DK)
            o, st = _gla_chunk(cq_ref[r, :, sl], cf_ref[r, :, sl], ci_ref[r, :, sl], lb[:, sl], s0_ref[r, h].T, ts)
            so_ref[r, h] = st.T
            yc_ref[r, :, sl] = _hgrn_out(o, cg_ref[r, :, sl], ng[:, sl])


def _sample_hgrn(cq, cf, ci, cg, states, lb, ng, ts, layer, carried):
    Bd, rows, _ = cq.shape
    depth = states.shape[0]
    rps = math.gcd(Bd, 4)
    req = pl.BlockSpec((rps, rows, BRANCH_W), lambda b: (b, 0, 0))
    st_block = (rps, C_HEADS, C_DK, C_DK)
    st_in = pl.BlockSpec((None,) + st_block, lambda b: (layer, b, 0, 0, 0))
    st_out, st_shape = _layer_slot(layer, depth, states.shape[1:], st_block, lambda b: (b, 0, 0, 0))
    vec = pl.BlockSpec((1, BRANCH_W), lambda b: (0, 0))
    carried = [] if carried is None else [carried]
    n_in = 7
    body = functools.partial(_shgrn_body, ts=ts)
    if carried:
        body = lambda *refs, inner=body: inner(*refs[:n_in], *refs[n_in + 1:])
    return pl.pallas_call(
        body,
        grid=(Bd // rps,),
        in_specs=[req, req, req, req, st_in, vec, vec] + [pl.BlockSpec(memory_space=pl.ANY)] * len(carried),
        out_specs=[req, st_out],
        out_shape=[jax.ShapeDtypeStruct((Bd, rows, BRANCH_W), F32), st_shape],
        input_output_aliases={n_in: 1} if carried else {},
        compiler_params=_cp(("parallel",)),
        name="hgrn_sample",
    )(cq, cf, ci, cg, states, lb, ng, *carried)


def _prompt_layer(x, B, T, w, lb, tables, layer, depth, kv_carried):
    tm = min(T, 1024)
    proj = _matmul(x, w["w_mix"], tm, N_MIX // 2, "proj_prompt")
    gates = _matmul(x, w["w_gate"], tm, N_GATE // 2, "gate_prompt", gate=True)
    kt, vtf, kb, qt, vt, bias = _attn_prep(proj, tables, B, T, layer, depth, kv_carried)
    ya = _prompt_attention(qt, kb, vt, bias, B, T)
    yb, yd, ztail = _prompt_mixers(proj, w["pool_w"], w["pool_scale"], w["sconv_w"], B, T, min(T, 512))
    yc, s_new = _prompt_hgrn(proj, lb, w["hgrn_norm_g"], B, T, min(T, 512))
    h = _merge((ya, yb, yc, yd), gates, x, w["w_branch"], w["w_o"], w["ln1_g"], w["ln1_b"], 256)
    act, up_tail = _prompt_ffn_up(h, w["ffn_up"], w["ffn_conv"], B, T, 256)
    out = _prompt_ffn_down(act, w["ffn_down"], h, w["ln2_g"], w["ln2_b"], 512)
    proj3 = proj.reshape(B, T, N_MIX)
    new_pool = proj3[:, T - POOL_BUF:, CB_PU * BRANCH_W:(CB_PU + 1) * BRANCH_W]
    new_sconv = ztail[:, 8 - (SCONV_W - 1):]
    new_ffn = up_tail[:, 8 - (FFN_CONV_W - 1):]
    return out, (kt, vtf), new_pool, s_new, new_sconv, new_ffn


def _to_tm(a):
    return a.transpose(1, 0, 2).reshape(a.shape[1] * a.shape[0], a.shape[2])


def _from_tm(a, Bd):
    return a.reshape(a.shape[0] // Bd, Bd, a.shape[1]).transpose(1, 0, 2)


def _sample_layer(x, Bd, Ts, pos0, layer, w, lb, tables, cache_k, cache_v, page_table, pool_buf, hgrn_states, hgrn_carried,
                  sconv_buf, ffn_buf):
    M = Ts * Bd
    proj = _matmul(x, w["w_mix"], M, BRANCH_W, "proj_sample")
    gates = _matmul(x, w["w_gate"], M, BRANCH_W, "gate_sample", gate=True)
    qf, _, kf, _, _ = _rope(proj, tables, M)
    col = lambda cb: proj[:, cb * BRANCH_W:(cb + 1) * BRANCH_W]
    q_r, k_r, v_r = _from_tm(qf, Bd), _from_tm(kf, Bd), _from_tm(col(CB_AV), Bd)
    head_of_lane = jnp.arange(BRANCH_W) // A_DH
    hmask = (head_of_lane[None, :] == jnp.arange(A_HEADS)[:, None]).astype(F32)
    qrows = (q_r[:, :, None, :] * hmask[None, None]).reshape(Bd, Ts * A_HEADS, BRANCH_W)
    pad8 = lambda a: jnp.pad(a, ((0, 0), (0, 8 - Ts), (0, 0)))
    ya = _sample_attention(qrows, pad8(k_r), pad8(v_r), cache_k, cache_v, page_table, layer, pos0, Ts)
    ufull = jnp.concatenate([_to_tm(pool_buf), col(CB_PU)], axis=0)
    yb, yd, znew = _sample_mixers(ufull, proj, _to_tm(sconv_buf), w["pool_w"], w["pool_scale"], w["sconv_w"], Bd, Ts, pos0)
    req8 = lambda cb: pad8(_from_tm(col(cb), Bd))
    yc, s_new = _sample_hgrn(req8(CB_CQ), req8(CB_CF), req8(CB_CI), req8(CB_CG), hgrn_states, lb, w["hgrn_norm_g"], Ts,
                             layer, hgrn_carried)
    ys = (_to_tm(ya), yb, _to_tm(yc[:, :Ts]), yd)
    h = _merge(ys, gates, x, w["w_branch"], w["w_o"], w["ln1_g"], w["ln1_b"], min(M, 256))
    up = _matmul(h, w["ffn_up"], M, BRANCH_W, "ffn_up_sample")
    full = jnp.concatenate([_to_tm(ffn_buf), up], axis=0)
    out = _sample_ffn(full, w["ffn_conv"], w["ffn_down"], h, w["ln2_g"], w["ln2_b"], Bd, Ts)
    k = k_r.reshape(Bd, Ts, A_HEADS, A_DH)
    v = v_r.reshape(Bd, Ts, A_HEADS, A_DH)
    new_pool = _from_tm(ufull[Ts * Bd:], Bd)
    new_sconv = _from_tm(znew, Bd)
    new_ffn = _from_tm(full[Ts * Bd:], Bd)
    return out, k, v, new_pool, s_new, new_sconv, new_ffn


def _layer_weights(l, w_in, w_branch, w_o, pool_w, pool_scale, hgrn_norm_g, sconv_w, ln1_g, ln1_b, ffn_up, ffn_conv,
                   ffn_down, ln2_g, ln2_b):
    row = lambda a: a[l][None, :]
    return {
        "w_mix": w_in[l][:, :N_MIX].astype(BF), "w_gate": w_in[l][:, N_MIX:].astype(BF),
        "w_branch": w_branch[l].astype(BF), "w_o": w_o[l].astype(BF),
        "pool_w": pool_w[l].astype(BF), "pool_scale": row(pool_scale), "hgrn_norm_g": row(hgrn_norm_g),
        "sconv_w": sconv_w[l], "ln1_g": row(ln1_g), "ln1_b": row(ln1_b), "ffn_up": ffn_up[l].astype(BF),
        "ffn_conv": ffn_conv[l], "ffn_down": ffn_down[l].astype(BF), "ln2_g": row(ln2_g), "ln2_b": row(ln2_b),
    }


def kernel(x_prompt, x_sample, cache_k, cache_v, state_pool, state_hgrn, state_sconv, state_ffn, page_table, w_in,
           w_branch, w_o, pool_w, pool_scale, hgrn_lb_logits, hgrn_norm_g, sconv_w, ln1_g, ln1_b, ffn_up, ffn_conv,
           ffn_down, ln2_g, ln2_b):
    depth = w_in.shape[0]
    Bp, T, _ = x_prompt.shape
    Bd, Ts, _ = x_sample.shape
    n_pages = page_table.shape[1]
    pos0 = n_pages * PAGE_SIZE
    assert T % MOBA_BLOCK == 0 and pos0 % MOBA_BLOCK == 0 and Ts <= 8 and T >= HALO
    lb_all = _lower_bounds(hgrn_lb_logits.astype(F32))
    tab_p = _rope_tables(jnp.arange(T))
    tab_s = _rope_tables(pos0 + jnp.repeat(jnp.arange(Ts), Bd))
    ck = cache_k.transpose(0, 1, 3, 4, 2).reshape(depth, cache_k.shape[1], BRANCH_W, PAGE_SIZE)
    cv = cache_v.transpose(0, 1, 3, 4, 2).reshape(depth, cache_v.shape[1], BRANCH_W, PAGE_SIZE)
    hp = x_prompt.reshape(Bp * T, D_MODEL)
    hs = _to_tm(x_sample)
    outs_p, outs_s = [], []
    kv_p = hgrn_s = None
    for l in range(depth):
        w = _layer_weights(l, w_in, w_branch, w_o, pool_w, pool_scale, hgrn_norm_g, sconv_w, ln1_g, ln1_b, ffn_up,
                           ffn_conv, ffn_down, ln2_g, ln2_b)
        lb = lb_all[l][None, :]
        rp = _prompt_layer(hp, Bp, T, w, lb, tab_p, l, depth, kv_p)
        rs = _sample_layer(hs, Bd, Ts, pos0, l, w, lb, tab_s, ck, cv, page_table, state_pool[l], state_hgrn, hgrn_s,
                           state_sconv[l], state_ffn[l])
        hp, kv_p = rp[0], rp[1]
        hs, hgrn_s = rs[0], rs[4]
        outs_p.append(rp[2:])
        outs_s.append(rs[1:4] + rs[5:])
    stack = lambda outs: [jnp.stack([o[j] for o in outs]) for j in range(len(outs[0]))]
    tokens_first = lambda a: a.reshape(depth, Bp, A_HEADS, A_DH, T).transpose(0, 1, 4, 2, 3)
    pool_p, hgrn_p, sconv_p, ffn_p = stack(outs_p)
    k_s, v_s, pool_s, sconv_s, ffn_s = stack(outs_s)
    return (hp.reshape(Bp, T, D_MODEL), _from_tm(hs, Bd), tokens_first(kv_p[0]), tokens_first(kv_p[1]), pool_p, hgrn_p,
            sconv_p, ffn_p, k_s, v_s, pool_s, hgrn_s, sconv_s, ffn_s)
```

```python
import functools
import math

import jax
import jax.numpy as jnp
import numpy as np
from jax import lax
from jax.experimental import pallas as pl
from jax.experimental.pallas import tpu as pltpu

F32 = jnp.float32
BF = jnp.bfloat16
NEG_INF = float("-inf")
LOG2_E = math.log2(math.e)

D_MODEL = 1024
BRANCH_W = 512
N_BRANCH = 4
A_HEADS = 8
A_DH = 64
MOBA_BLOCK = 256
MOBA_TOPK = 3
ROPE_DIM = A_DH // 4
ROPE_THETA = 500000.0
POOL_WINDOWS = (2, 4, 8, 16)
POOL_GC = 128
POOL_BUF = 15
C_HEADS = 4
C_DK = 128
SCONV_W = 3
D_FF = 2816
FFN_CONV_W = 3
PAGE_SIZE = 128
DEPTH = 4
DN_ALPHA = (2 * DEPTH) ** 0.25
LN_EPS = 1e-5
RMS_EPS = 1e-6

CB_AQ, CB_AK, CB_AV, CB_PU, CB_CQ, CB_CF, CB_CI, CB_CG, CB_DB, CB_DC, CB_DH = range(11)
N_MIX = 11 * BRANCH_W
N_GATE = N_BRANCH * D_MODEL

LANES = 128
HALO = 16
SUM_ROWS = 16
ATTN_HEADS_PER_STEP = 4
VMEM_LIMIT = 56 * 1024 * 1024

NT = (((1,), (1,)), ((), ()))
TN = (((0,), (0,)), ((), ()))


def _cp(sem, vmem=VMEM_LIMIT):
    return pltpu.CompilerParams(dimension_semantics=sem, vmem_limit_bytes=vmem)


def _sigmoid(x):
    return 1.0 / (1.0 + jnp.exp(-x))


def _silu(x):
    return x * _sigmoid(x)


def _layer_norm(x, g, b):
    mu = jnp.mean(x, axis=-1, keepdims=True)
    xc = x - mu
    var = jnp.mean(xc * xc, axis=-1, keepdims=True)
    return xc * lax.rsqrt(var + LN_EPS) * g + b


def _lb_body(logit_ref, o_ref):
    x = logit_ref[...]
    m = jnp.max(x, axis=0, keepdims=True)
    e = jnp.exp(x - m)
    p = e / jnp.sum(e, axis=0, keepdims=True)
    acc = jnp.zeros_like(p[0:1])
    rows = []
    for l in range(x.shape[0]):
        acc = acc + p[l:l + 1]
        rows.append(acc - p[0:1])
    o_ref[...] = jnp.concatenate(rows, axis=0)


def _lower_bounds(logits):
    return pl.pallas_call(_lb_body, out_shape=jax.ShapeDtypeStruct(logits.shape, F32), name="hgrn_lb")(logits)


def _mm_body(x_ref, w_ref, o_ref, xb_ref, *, gate):
    @pl.when(pl.program_id(1) == 0)
    def _():
        xb_ref[...] = x_ref[...].astype(BF)

    y = jnp.dot(xb_ref[...], w_ref[...], preferred_element_type=F32)
    o_ref[...] = (_sigmoid(y) if gate else y).astype(o_ref.dtype)


def _matmul(x, w_bf, tm, tn, name, gate=False):
    M, K = x.shape
    N = w_bf.shape[1]
    assert M % tm == 0 and N % tn == 0
    return pl.pallas_call(
        functools.partial(_mm_body, gate=gate),
        grid=(M // tm, N // tn),
        in_specs=[pl.BlockSpec((tm, K), lambda i, j: (i, 0)),
                  pl.BlockSpec((K, tn), lambda i, j: (0, j))],
        out_specs=pl.BlockSpec((tm, tn), lambda i, j: (i, j)),
        out_shape=jax.ShapeDtypeStruct((M, N), BF if gate else F32),
        scratch_shapes=[pltpu.VMEM((tm, K), BF)],
        compiler_params=_cp(("parallel", "arbitrary")),
        name=name,
    )(x, w_bf)


def _rope_tables(pos):
    half = ROPE_DIM // 2
    inv = jnp.power(ROPE_THETA, -jnp.arange(0, ROPE_DIM, 2, dtype=F32) / ROPE_DIM)
    ang = pos.astype(F32)[:, None] * inv[None, :]
    cos, sin = jnp.cos(ang), jnp.sin(ang)
    n = pos.shape[0]
    ones = jnp.ones((n, A_DH - ROPE_DIM), F32)
    zeros = jnp.zeros((n, A_DH - ROPE_DIM), F32)
    zh = jnp.zeros((n, half), F32)
    c = jnp.concatenate([cos, cos, ones], axis=1)
    s_lo = jnp.concatenate([-sin, zh, zeros], axis=1)
    s_hi = jnp.concatenate([zh, sin, zeros], axis=1)
    tile = lambda a: jnp.tile(a, (1, A_HEADS))
    return tile(c), tile(s_lo), tile(s_hi)


def _rope_body(q_ref, k_ref, v_ref, c_ref, sl_ref, sh_ref, qf_ref, qb_ref, kf_ref, kb_ref, vb_ref):
    c, sl, sh = c_ref[...], sl_ref[...], sh_ref[...]
    half = ROPE_DIM // 2
    w = q_ref.shape[1]

    def rope(x):
        return x * c + pltpu.roll(x, w - half, 1) * sl + pltpu.roll(x, half, 1) * sh

    q = rope(q_ref[...])
    k = rope(k_ref[...])
    qf_ref[...] = q
    qb_ref[...] = (q * (A_DH ** -0.5)).astype(BF)
    kf_ref[...] = k
    kb_ref[...] = k.astype(BF)
    vb_ref[...] = v_ref[...].astype(BF)


def _rope(proj, tables, tm):
    M = proj.shape[0]
    ntab = tables[0].shape[0] // tm
    blk = lambda cb: pl.BlockSpec((tm, BRANCH_W), lambda i, cb=cb: (i, cb))
    tab = pl.BlockSpec((tm, BRANCH_W), lambda i: (i % ntab, 0))
    out = pl.BlockSpec((tm, BRANCH_W), lambda i: (i, 0))
    sds = lambda dt: jax.ShapeDtypeStruct((M, BRANCH_W), dt)
    return pl.pallas_call(
        _rope_body,
        grid=(M // tm,),
        in_specs=[blk(CB_AQ), blk(CB_AK), blk(CB_AV), tab, tab, tab],
        out_specs=[out] * 5,
        out_shape=[sds(F32), sds(BF), sds(F32), sds(BF), sds(BF)],
        compiler_params=_cp(("parallel",)),
        name="rope",
    )(proj, proj, proj, *tables)


def _top_blocks_bias(gate, n_past):
    col = lax.broadcasted_iota(jnp.int32, gate.shape, 1)
    g = jnp.where(col < n_past, gate, NEG_INF)
    bias = jnp.full(gate.shape, NEG_INF, F32)
    for _ in range(MOBA_TOPK):
        m = jnp.max(g, axis=1, keepdims=True)
        idx = jnp.min(jnp.where(g == m, col, LANES), axis=1, keepdims=True)
        pick = (col == idx) & (m > NEG_INF)
        bias = jnp.where(pick, 0.0, bias)
        g = jnp.where(pick, NEG_INF, g)
    return bias


def _split_bf16(x):
    hi = x.astype(BF)
    return hi, (x - hi.astype(F32)).astype(BF)


def _prep_body(qkv_ref, c_ref, sl_ref, sh_ref, kt_ref, vtf_ref, kb_ref, qt_ref, vt_ref, bias_ref, km_ref):
    i = pl.program_id(1)
    nbp = km_ref.shape[0]
    q_ref, k_ref, v_ref = (qkv_ref.at[:, n * BRANCH_W:(n + 1) * BRANCH_W] for n in range(3))

    @pl.when(i == 0)
    def _():
        km_ref[...] = jnp.zeros_like(km_ref)

    c, sl, sh = c_ref[...], sl_ref[...], sh_ref[...]
    half = ROPE_DIM // 2

    def rope(x):
        return x * c + pltpu.roll(x, BRANCH_W - half, 1) * sl + pltpu.roll(x, half, 1) * sh

    q = rope(q_ref[...])
    k = rope(k_ref[...])
    kt_ref[...] = k.T
    kb_ref[...] = k.astype(BF)
    qt_ref[...] = (q * (A_DH ** -0.5 * LOG2_E)).T.astype(BF)
    v_t = v_ref[...].T
    vtf_ref[...] = v_t
    vt_ref[...] = v_t.astype(BF)

    km = km_ref[...]
    lane = lax.broadcasted_iota(jnp.int32, km.shape, 1)
    stack = jnp.concatenate([jnp.where(lane // A_DH == h, km, 0.0) for h in range(A_HEADS)], axis=0)
    k_hi, k_lo = _split_bf16(stack)
    q_hi, q_lo = _split_bf16(q)
    dot = lambda a, b: lax.dot_general(a, b, NT, preferred_element_type=F32)
    gate = (dot(k_hi, q_hi) + dot(k_lo, q_hi) + dot(k_hi, q_lo)).reshape(A_HEADS, nbp, MOBA_BLOCK)
    blk = lax.broadcasted_iota(jnp.int32, gate.shape, 1)
    gate = jnp.where(blk < i, gate, NEG_INF)
    bias = jnp.full(gate.shape, NEG_INF, F32)
    for _ in range(MOBA_TOPK):
        m = jnp.max(gate, axis=1, keepdims=True)
        idx = jnp.min(jnp.where(gate == m, blk, nbp), axis=1, keepdims=True)
        pick = (blk == idx) & (m > NEG_INF)
        bias = jnp.where(pick, 0.0, bias)
        gate = jnp.where(pick, NEG_INF, gate)
    bias_ref[0] = bias.reshape(A_HEADS * nbp, MOBA_BLOCK)
    km_ref[pl.ds(i, 1), :] = jnp.sum(k, axis=0, keepdims=True) * (1.0 / MOBA_BLOCK)


def _layer_slot(layer, depth, shape, block, index_map):
    spec = pl.BlockSpec((None,) + block, lambda *a: (layer,) + tuple(index_map(*a)))
    return spec, jax.ShapeDtypeStruct((depth,) + shape, F32)


def _attn_prep(proj, tables, B, T, layer, depth, kv_carried):
    nq = T // MOBA_BLOCK
    nbp = -(-nq // 8) * 8
    tq = MOBA_BLOCK
    assert (CB_AQ, CB_AK, CB_AV) == tuple(range(CB_AQ, CB_AQ + 3)) and CB_AQ % 3 == 0
    qkv = pl.BlockSpec((tq, 3 * BRANCH_W), lambda b, i: (b * nq + i, CB_AQ // 3))
    tab = pl.BlockSpec((tq, BRANCH_W), lambda b, i: (i, 0))
    row = pl.BlockSpec((tq, BRANCH_W), lambda b, i: (b * nq + i, 0))
    tr = pl.BlockSpec((BRANCH_W, tq), lambda b, i: (b, i))
    slot, slot_shape = _layer_slot(layer, depth, (B * BRANCH_W, T), (BRANCH_W, tq), lambda b, i: (b, i))
    carried = [] if kv_carried is None else list(kv_carried)
    n_in = 4
    body = _prep_body if not carried else (lambda *refs: _prep_body(*refs[:n_in], *refs[n_in + len(carried):]))
    return pl.pallas_call(
        body,
        grid=(B, nq),
        in_specs=[qkv, tab, tab, tab] + [pl.BlockSpec(memory_space=pl.ANY)] * len(carried),
        out_specs=[slot, slot, row, tr, tr, pl.BlockSpec((1, A_HEADS * nbp, tq), lambda b, i: (b, 0, i))],
        out_shape=[slot_shape, slot_shape,
                   jax.ShapeDtypeStruct((B * T, BRANCH_W), BF),
                   jax.ShapeDtypeStruct((B * BRANCH_W, T), BF), jax.ShapeDtypeStruct((B * BRANCH_W, T), BF),
                   jax.ShapeDtypeStruct((B, A_HEADS * nbp, T), F32)],
        input_output_aliases={n_in + n: n for n in range(len(carried))},
        scratch_shapes=[pltpu.VMEM((nbp, BRANCH_W), F32)],
        compiler_params=_cp(("parallel", "arbitrary")),
        name="attn_prep",
    )(proj, *tables, *carried)


def _attn_body(qt_ref, k_ref, vt_ref, bias_ref, o_ref, sa_ref, sb_ref, pp_ref):
    i = pl.program_id(2)
    tq = MOBA_BLOCK
    npair = qt_ref.shape[0] // A_DH
    nbp = bias_ref.shape[1] // npair
    qt = qt_ref[...]
    rowi = lax.broadcasted_iota(jnp.int32, qt.shape, 0)
    qts = [jnp.where((rowi >= e * A_DH) & (rowi < (e + 1) * A_DH), qt, jnp.zeros_like(qt)) for e in range(npair)]
    block = lambda j: pl.ds(j * tq if isinstance(j, int) else pl.multiple_of(j * tq, tq), tq)
    ones = jnp.ones((SUM_ROWS, tq), BF)

    def scores(s_ref, j):
        kj = k_ref[block(j), :]
        for e in range(npair):
            s_ref[e] = jnp.dot(kj, qts[e], preferred_element_type=F32)

    def pending_product(e, jp):
        vt = jnp.concatenate([vt_ref[e * A_DH:(e + 1) * A_DH, block(jp)], ones], axis=0)
        return jnp.dot(vt, pp_ref[e], preferred_element_type=F32)

    def settle(e, head, jp):
        m, l, acc, pa = head
        pv = pending_product(e, jp)
        return m, pa * l + pv[A_DH:A_DH + 1], pa * acc + pv[:A_DH]

    def consume(s_ref, j, carry, own):
        heads, jp = carry
        new = []
        for e in range(npair):
            m, l, acc = settle(e, heads[e], jp)
            s = s_ref[e]
            if own:
                causal = (lax.broadcasted_iota(jnp.int32, (tq, tq), 0) <= lax.broadcasted_iota(jnp.int32, (tq, tq), 1))
                s = jnp.where(causal, s, NEG_INF)
                mn = jnp.maximum(m, jnp.max(s, axis=0, keepdims=True))
                shift = mn
            else:
                b = bias_ref[0, pl.ds(e * nbp + j, 1), :]
                mn = jnp.maximum(m, jnp.max(s, axis=0, keepdims=True) + b)
                shift = mn - b
            pp_ref[e] = jnp.exp2(s - shift).astype(BF)
            new.append((mn, l, acc, jnp.exp2(m - mn)))
        return tuple(new), j

    def two_blocks(p, carry):
        scores(sb_ref, 2 * p + 1)
        carry = consume(sa_ref, 2 * p, carry, own=False)
        scores(sa_ref, 2 * p + 2)
        return consume(sb_ref, 2 * p + 1, carry, own=False)

    def odd_tail(carry):
        scores(sb_ref, i)
        carry = consume(sa_ref, i - 1, carry, own=False)
        return consume(sb_ref, i, carry, own=True)

    def even_tail(carry):
        return consume(sa_ref, i, carry, own=True)

    pp_ref[...] = jnp.zeros_like(pp_ref)
    init = tuple((jnp.full((1, tq), -1e30, F32), jnp.zeros((1, tq), F32), jnp.zeros((A_DH, tq), F32),
                  jnp.ones((1, tq), F32)) for _ in range(npair))
    scores(sa_ref, 0)
    state = lax.fori_loop(0, i // 2, two_blocks, (init, jnp.int32(0)))
    heads, jp = lax.cond(i % 2 == 1, odd_tail, even_tail, state)
    outs = []
    for e in range(npair):
        _, l, acc = settle(e, heads[e], jp)
        outs.append(acc / l)
    o_ref[...] = jnp.concatenate(outs, axis=0).T.astype(o_ref.dtype)


def _prompt_attention(qt, kb, vt, bias, B, T):
    nq = T // MOBA_BLOCK
    npair = ATTN_HEADS_PER_STEP
    width = npair * A_DH
    ngrp = BRANCH_W // width
    nbp = bias.shape[1] // A_HEADS
    return pl.pallas_call(
        _attn_body,
        grid=(B, ngrp, nq),
        in_specs=[pl.BlockSpec((width, MOBA_BLOCK), lambda b, hp, i: (b * ngrp + hp, i)),
                  pl.BlockSpec((T, width), lambda b, hp, i: (b, hp)),
                  pl.BlockSpec((width, T), lambda b, hp, i: (b * ngrp + hp, 0)),
                  pl.BlockSpec((1, npair * nbp, MOBA_BLOCK), lambda b, hp, i: (b, hp, i))],
        out_specs=pl.BlockSpec((MOBA_BLOCK, width), lambda b, hp, i: (b * nq + i, hp)),
        out_shape=jax.ShapeDtypeStruct((B * T, BRANCH_W), BF),
        scratch_shapes=[pltpu.VMEM((npair, MOBA_BLOCK, MOBA_BLOCK), F32)] * 2
        + [pltpu.VMEM((npair, MOBA_BLOCK, MOBA_BLOCK), BF)],
        compiler_params=_cp(("parallel", "parallel", "arbitrary")),
        name="moba_prompt",
    )(qt, kb, vt, bias)


def _pool_group(win, u, g, w, t_first, pw, ps):
    tm = u.shape[0]
    sl = slice(g * POOL_GC, (g + 1) * POOL_GC)
    t = t_first + lax.broadcasted_iota(jnp.int32, (tm, POOL_GC), 0)
    cnt = jnp.minimum(w, t + 1).astype(F32)
    pooled = win / cnt - u[:, sl]
    return jnp.dot(pooled.astype(BF), pw, preferred_element_type=F32) * ps[:, sl]


def _mix_body(u_ref, uh_ref, db_ref, dc_ref, dh_ref, dch_ref, dhh_ref, pw_ref, ps_ref, sw_ref,
              yb_ref, yd_ref, zt_ref, ext_ref, zext_ref, s2_ref, s4_ref, s8_ref):
    assert POOL_WINDOWS == (2, 4, 8, 16)
    i = pl.program_id(1)
    tm = u_ref.shape[0]
    first = i == 0
    u = u_ref[...]
    base = 8 + HALO
    rows = base + tm
    gc = POOL_GC
    zero8 = jnp.zeros((8, BRANCH_W), F32)
    ext_ref[0:8, :] = zero8
    ext_ref[8:base, :] = jnp.where(first, 0.0, uh_ref[...])
    ext_ref[base:, :] = u
    s2_ref[0:8, :] = zero8
    s2_ref[8:rows, :] = ext_ref[8:rows, :] + ext_ref[7:rows - 1, :]
    s4_ref[0:8, :] = zero8
    s4_ref[8:rows, gc:] = s2_ref[8:rows, gc:] + s2_ref[6:rows - 2, gc:]
    s8_ref[0:8, :] = zero8
    s8_ref[8:rows, 2 * gc:] = s4_ref[8:rows, 2 * gc:] + s4_ref[4:rows - 4, 2 * gc:]
    tile = slice(base, rows)
    wins = (s2_ref[tile, 0:gc], s4_ref[tile, gc:2 * gc], s8_ref[tile, 2 * gc:3 * gc],
            s8_ref[tile, 3 * gc:] + s8_ref[base - 8:rows - 8, 3 * gc:])
    ps = ps_ref[...]
    for g, w in enumerate(POOL_WINDOWS):
        y = _pool_group(wins[g], u, g, w, i * tm, pw_ref[g], ps)
        yb_ref[:, g * POOL_GC:(g + 1) * POOL_GC] = y.astype(yb_ref.dtype)
    z = dc_ref[...] * dh_ref[...]
    zext_ref[0:8, :] = jnp.where(first, 0.0, dch_ref[...] * dhh_ref[...])
    zext_ref[8:, :] = z
    sw = sw_ref[...]
    conv = sw[0:1] * zext_ref[6:6 + tm, :] + sw[1:2] * zext_ref[7:7 + tm, :] + sw[2:3] * z
    yd_ref[...] = (db_ref[...] * conv).astype(yd_ref.dtype)
    zt_ref[0] = z[tm - 8:tm]


def _prompt_mixers(proj, pool_w_bf, pool_scale, sconv_w, B, T, tm):
    nt = T // tm
    row = lambda b, i: b * nt + i
    blk = lambda cb: pl.BlockSpec((tm, BRANCH_W), lambda b, i, cb=cb: (row(b, i), cb))

    def halo(cb, h):
        return pl.BlockSpec((h, BRANCH_W), lambda b, i, cb=cb, h=h: (jnp.maximum(row(b, i) * (tm // h) - 1, 0), cb))

    const = lambda shape: pl.BlockSpec(shape, lambda b, i, n=len(shape): (0,) * n)
    return pl.pallas_call(
        _mix_body,
        grid=(B, nt),
        in_specs=[blk(CB_PU), halo(CB_PU, HALO), blk(CB_DB), blk(CB_DC), blk(CB_DH), halo(CB_DC, 8), halo(CB_DH, 8),
                  const((len(POOL_WINDOWS), POOL_GC, POOL_GC)), const((1, BRANCH_W)), const((SCONV_W, BRANCH_W))],
        out_specs=[pl.BlockSpec((tm, BRANCH_W), lambda b, i: (row(b, i), 0)),
                   pl.BlockSpec((tm, BRANCH_W), lambda b, i: (row(b, i), 0)),
                   pl.BlockSpec((1, 8, BRANCH_W), lambda b, i: (b, 0, 0))],
        out_shape=[jax.ShapeDtypeStruct((B * T, BRANCH_W), BF), jax.ShapeDtypeStruct((B * T, BRANCH_W), BF),
                   jax.ShapeDtypeStruct((B, 8, BRANCH_W), F32)],
        scratch_shapes=[pltpu.VMEM((tm + HALO + 8, BRANCH_W), F32), pltpu.VMEM((tm + 8, BRANCH_W), F32)]
        + [pltpu.VMEM((tm + HALO + 8, BRANCH_W), F32)] * 3,
        compiler_params=_cp(("parallel", "arbitrary")),
        name="pool_sconv_prompt",
    )(proj, proj, proj, proj, proj, proj, proj, pool_w_bf, pool_scale, sconv_w)


def _gla_chunk(cq, cf, ci, lb, st, n_valid):
    rows = cq.shape[0]
    f = lb + (1.0 - lb) * _sigmoid(cf)
    g = jnp.log(f)
    r_i = lax.broadcasted_iota(jnp.int32, (rows, rows), 0)
    c_i = lax.broadcasted_iota(jnp.int32, (rows, rows), 1)
    tri = jnp.where((r_i >= c_i) & (c_i < n_valid), 1.0, 0.0).astype(F32)
    b = jnp.dot(tri, g, precision=lax.Precision.HIGHEST, preferred_element_type=F32)
    qs = _silu(cq)
    kk = 1.0 - f
    t_i = lax.broadcasted_iota(jnp.int32, (rows, 1), 0)
    o = jnp.zeros((rows, C_DK), F32)
    for s in range(n_valid):
        e = jnp.exp(jnp.minimum(b - b[s:s + 1], 0.0))
        a = jnp.sum(qs * kk[s:s + 1] * e, axis=1, keepdims=True)
        o = o + jnp.where(t_i >= s, a, 0.0) * ci[s:s + 1]
    o = o + lax.dot_general((qs * jnp.exp(b)).astype(BF), st.astype(BF), NT, preferred_element_type=F32)
    bl = b[n_valid - 1:n_valid]
    ks = jnp.where(t_i < n_valid, kk * jnp.exp(jnp.minimum(bl - b, 0.0)), 0.0)
    upd = lax.dot_general(ci.astype(BF), ks.astype(BF), TN, preferred_element_type=F32)
    return o, st * jnp.exp(bl) + upd


def _hgrn_out(o, cg, ng):
    o = o * lax.rsqrt(jnp.mean(o * o, axis=1, keepdims=True) + RMS_EPS)
    return o * ng * _silu(cg)


GROUP = 128
GROUP_HALVES = tuple(GROUP >> (s + 1) for s in range(int(math.log2(GROUP))))


def _decay_sum_matrix():
    t = np.arange(GROUP)[:, None]
    u = np.arange(GROUP)[None, :]
    blocks = [u <= t, u > t]
    for h in GROUP_HALVES:
        r = (t // (2 * h)) * 2 * h + h - 1
        second = (t // h) % 2 == 1
        blocks.append(np.where(second, (u > r) & (u <= t), (u > t) & (u <= r)))
    d = np.concatenate(blocks, axis=0).astype(np.float32)
    return jnp.asarray(np.concatenate([d, d], axis=1), dtype=BF)


def _gla_decays(cf, lb, dmat):
    f = lb + (1.0 - lb) * _sigmoid(cf)
    g_hi, g_lo = _split_bf16(jnp.log(f))
    return f, jnp.dot(dmat, jnp.concatenate([g_hi, g_lo], axis=0), preferred_element_type=F32)


def _gla_group(cq, f, x, ci, st):
    e = jnp.exp(x)
    blk = lambda n: e[n * GROUP:(n + 1) * GROUP]
    qs = _silu(cq)
    kk = 1.0 - f
    v = ci.astype(BF)
    t_i = lax.broadcasted_iota(jnp.int32, (GROUP, GROUP), 0)
    s_i = lax.broadcasted_iota(jnp.int32, (GROUP, GROUP), 1)
    row = lax.broadcasted_iota(jnp.int32, (GROUP, C_DK), 0)
    dot_nt = lambda a, b: lax.dot_general(a.astype(BF), b.astype(BF), NT, preferred_element_type=F32)
    attn = jnp.where(t_i == s_i, dot_nt(qs, kk), 0.0)
    for s, h in enumerate(GROUP_HALVES):
        second = (row // h) % 2 == 1
        scaled_q = jnp.where(second, qs * blk(2 + s), 0.0)
        scaled_k = jnp.where(second, 0.0, kk * blk(2 + s))
        part = dot_nt(scaled_q, scaled_k)
        attn = attn + (part if 2 * h == GROUP else jnp.where(t_i // (2 * h) == s_i // (2 * h), part, 0.0))
    o = jnp.dot(attn.astype(BF), v, preferred_element_type=F32) + dot_nt(qs * blk(0), st)
    upd = lax.dot_general(v, (kk * blk(1)).astype(BF), TN, preferred_element_type=F32)
    return o, st * e[GROUP - 1:GROUP] + upd


def _hgrn_body(c_ref, lb_ref, ng_ref, dm_ref, yc_ref, so_ref, st_ref):
    it = pl.program_id(1)
    tc = c_ref.shape[0]
    cq_ref, cf_ref, ci_ref, cg_ref = (c_ref.at[:, n * BRANCH_W:(n + 1) * BRANCH_W] for n in range(4))

    @pl.when(it == 0)
    def _():
        st_ref[...] = jnp.zeros_like(st_ref)

    lb, ng = lb_ref[...], ng_ref[...]

    def group(gi, carry):
        rows = pl.ds(pl.multiple_of(gi * GROUP, GROUP), GROUP)
        head = lambda h: slice(h * C_DK, (h + 1) * C_DK)
        f, x = _gla_decays(cf_ref[rows, :], lb, dm_ref[...])
        for h in range(C_HEADS):
            sl = head(h)
            o, st = _gla_group(cq_ref[rows, sl], f[:, sl], x[:, sl], ci_ref[rows, sl], st_ref[h])
            st_ref[h] = st
            yc_ref[rows, sl] = _hgrn_out(o, cg_ref[rows, sl], ng[:, sl]).astype(yc_ref.dtype)
        return carry

    lax.fori_loop(0, tc // GROUP, group, 0)

    @pl.when(it == pl.num_programs(1) - 1)
    def _():
        for h in range(C_HEADS):
            so_ref[0, h] = st_ref[h].T


def _prompt_hgrn(proj, lb, ng, B, T, tc):
    nt = T // tc
    dmat = _decay_sum_matrix()
    assert (CB_CQ, CB_CF, CB_CI, CB_CG) == tuple(range(CB_CQ, CB_CQ + 4)) and CB_CQ % 4 == 0
    vec = pl.BlockSpec((1, BRANCH_W), lambda b, i: (0, 0))
    return pl.pallas_call(
        _hgrn_body,
        grid=(B, nt),
        in_specs=[pl.BlockSpec((tc, 4 * BRANCH_W), lambda b, i: (b * nt + i, CB_CQ // 4)), vec, vec,
                  pl.BlockSpec(dmat.shape, lambda b, i: (0, 0))],
        out_specs=[pl.BlockSpec((tc, BRANCH_W), lambda b, i: (b * nt + i, 0)),
                   pl.BlockSpec((1, C_HEADS, C_DK, C_DK), lambda b, i: (b, 0, 0, 0))],
        out_shape=[jax.ShapeDtypeStruct((B * T, BRANCH_W), BF), jax.ShapeDtypeStruct((B, C_HEADS, C_DK, C_DK), F32)],
        scratch_shapes=[pltpu.VMEM((C_HEADS, C_DK, C_DK), F32)],
        compiler_params=_cp(("parallel", "arbitrary")),
        name="hgrn_prompt",
    )(proj, lb, ng, dmat)


def _merge_body(ya_ref, yb_ref, yc_ref, yd_ref, g_ref, x_ref, wb_ref, wo_ref, lg_ref, lbias_ref, h_ref):
    mixed = jnp.zeros(x_ref.shape, F32)
    for n, y_ref in enumerate((ya_ref, yb_ref, yc_ref, yd_ref)):
        br = jnp.dot(y_ref[...].astype(BF), wb_ref[n], preferred_element_type=F32)
        mixed = mixed + g_ref[:, n * D_MODEL:(n + 1) * D_MODEL].astype(F32) * br
    pre = DN_ALPHA * x_ref[...] + jnp.dot(mixed.astype(BF), wo_ref[...], preferred_element_type=F32)
    h_ref[...] = _layer_norm(pre, lg_ref[...], lbias_ref[...])


def _merge(ys, gates, x, wb_bf, wo_bf, ln_g, ln_b, tm):
    M = x.shape[0]
    yspec = pl.BlockSpec((tm, BRANCH_W), lambda i: (i, 0))
    gspec = pl.BlockSpec((tm, N_GATE), lambda i: (i, 0))
    xspec = pl.BlockSpec((tm, D_MODEL), lambda i: (i, 0))
    const = lambda shape: pl.BlockSpec(shape, lambda i, n=len(shape): (0,) * n)
    return pl.pallas_call(
        _merge_body,
        grid=(M // tm,),
        in_specs=[yspec] * N_BRANCH + [gspec, xspec, const((N_BRANCH, BRANCH_W, D_MODEL)), const((D_MODEL, D_MODEL)),
                                       const((1, D_MODEL)), const((1, D_MODEL))],
        out_specs=xspec,
        out_shape=jax.ShapeDtypeStruct((M, D_MODEL), F32),
        compiler_params=_cp(("parallel",)),
        name="merge_ln1",
    )(*ys, gates, x, wb_bf, wo_bf, ln_g, ln_b)


def _ffn_finish(gate, val, wd_ref, h_ref, lg_ref, lb_ref, o_ref):
    a = (_silu(gate) * val).astype(BF)
    pre = DN_ALPHA * h_ref[...] + jnp.dot(a, wd_ref[...], preferred_element_type=F32)
    o_ref[...] = _layer_norm(pre, lg_ref[...], lb_ref[...])


FFN_COLS = 256


def _ffn_up_body(x_ref, w_ref, cw_ref, a_ref, tail_ref, halo_ref, *, tiles_per_seq):
    i = pl.program_id(0)
    tm = x_ref.shape[0]

    @pl.when(i == 0)
    def _():
        halo_ref[...] = jnp.zeros_like(halo_ref)

    xb = x_ref[...].astype(BF)
    seq_start = i % tiles_per_seq == 0

    def conv(cols):
        u = jnp.dot(xb, w_ref[:, cols], preferred_element_type=F32)
        prev = jnp.where(seq_start, 0.0, halo_ref[:, cols])
        halo_ref[:, cols] = u[tm - 8:tm]
        tail_ref[0, :, cols] = u[tm - 8:tm]
        cw = cw_ref[:, cols]
        back1 = jnp.concatenate([prev[7:8], u[:tm - 1]], axis=0)
        back2 = jnp.concatenate([prev[6:8], u[:tm - 2]], axis=0)
        return cw[0:1] * back2 + cw[1:2] * back1 + cw[2:3] * u

    for c in range(D_FF // FFN_COLS):
        gate = conv(slice(c * FFN_COLS, (c + 1) * FFN_COLS))
        val = conv(slice(D_FF + c * FFN_COLS, D_FF + (c + 1) * FFN_COLS))
        a_ref[:, c * FFN_COLS:(c + 1) * FFN_COLS] = (_silu(gate) * val).astype(a_ref.dtype)


def _prompt_ffn_up(h, wu_bf, ffn_conv, B, T, tm):
    M, K = h.shape
    nt = T // tm
    assert D_FF % FFN_COLS == 0
    const = lambda shape: pl.BlockSpec(shape, lambda i, n=len(shape): (0,) * n)
    return pl.pallas_call(
        functools.partial(_ffn_up_body, tiles_per_seq=nt),
        grid=(M // tm,),
        in_specs=[pl.BlockSpec((tm, K), lambda i: (i, 0)), const((K, 2 * D_FF)), const((FFN_CONV_W, 2 * D_FF))],
        out_specs=[pl.BlockSpec((tm, D_FF), lambda i: (i, 0)),
                   pl.BlockSpec((1, 8, 2 * D_FF), lambda i: (i // nt, 0, 0))],
        out_shape=[jax.ShapeDtypeStruct((M, D_FF), BF), jax.ShapeDtypeStruct((B, 8, 2 * D_FF), F32)],
        scratch_shapes=[pltpu.VMEM((8, 2 * D_FF), F32)],
        compiler_params=_cp(("arbitrary",)),
        name="ffn_up_act_prompt",
    )(h, wu_bf, ffn_conv)


def _ffn_down_body(a_ref, wd_ref, h_ref, lg_ref, lb_ref, o_ref):
    pre = DN_ALPHA * h_ref[...] + jnp.dot(a_ref[...], wd_ref[...], preferred_element_type=F32)
    o_ref[...] = _layer_norm(pre, lg_ref[...], lb_ref[...])


def _prompt_ffn_down(a, wd_bf, h, ln_g, ln_b, tm):
    M = h.shape[0]
    const = lambda shape: pl.BlockSpec(shape, lambda i, n=len(shape): (0,) * n)
    xspec = pl.BlockSpec((tm, D_MODEL), lambda i: (i, 0))
    return pl.pallas_call(
        _ffn_down_body,
        grid=(M // tm,),
        in_specs=[pl.BlockSpec((tm, D_FF), lambda i: (i, 0)), const((D_FF, D_MODEL)), xspec,
                  const((1, D_MODEL)), const((1, D_MODEL))],
        out_specs=xspec,
        out_shape=jax.ShapeDtypeStruct((M, D_MODEL), F32),
        compiler_params=_cp(("parallel",)),
        name="ffn_down_prompt",
    )(a, wd_bf, h, ln_g, ln_b)


def _ffn_sample_body(g0_ref, g1_ref, g2_ref, v0_ref, v1_ref, v2_ref, cwg_ref, cwv_ref, wd_ref, h_ref, lg_ref, lb_ref, o_ref):
    cwg, cwv = cwg_ref[...], cwv_ref[...]
    gate = cwg[0:1] * g0_ref[...] + cwg[1:2] * g1_ref[...] + cwg[2:3] * g2_ref[...]
    val = cwv[0:1] * v0_ref[...] + cwv[1:2] * v1_ref[...] + cwv[2:3] * v2_ref[...]
    _ffn_finish(gate, val, wd_ref, h_ref, lg_ref, lb_ref, o_ref)


def _sample_ffn(full, ffn_conv, wd_bf, h, ln_g, ln_b, Bd, Ts):
    blk = lambda c, r: pl.BlockSpec((Bd, D_FF), lambda t, c=c, r=r: (t + r, c))
    cw = lambda c: pl.BlockSpec((FFN_CONV_W, D_FF), lambda t, c=c: (0, c))
    const = lambda shape: pl.BlockSpec(shape, lambda t, n=len(shape): (0,) * n)
    xspec = pl.BlockSpec((Bd, D_MODEL), lambda t: (t, 0))
    return pl.pallas_call(
        _ffn_sample_body,
        grid=(Ts,),
        in_specs=[blk(0, 0), blk(0, 1), blk(0, 2), blk(1, 0), blk(1, 1), blk(1, 2), cw(0), cw(1),
                  const((D_FF, D_MODEL)), xspec, const((1, D_MODEL)), const((1, D_MODEL))],
        out_specs=xspec,
        out_shape=jax.ShapeDtypeStruct((Ts * Bd, D_MODEL), F32),
        compiler_params=_cp(("parallel",)),
        name="ffn_tail_sample",
    )(full, full, full, full, full, full, ffn_conv, ffn_conv, wd_bf, h, ln_g, ln_b)


REQS_PER_STEP = 2


def _sattn_body(pt_ref, q_ref, kn_ref, vn_ref, *rest, n_pages, pos0, ts):
    del pt_ref
    o_ref = rest[2 * n_pages * REQS_PER_STEP]
    for r in range(REQS_PER_STEP):
        pages = rest[2 * n_pages * r:2 * n_pages * (r + 1)]
        o_ref[r] = _sattn_request(q_ref[r], kn_ref[r], vn_ref[r], pages[:n_pages], pages[n_pages:], pos0, ts)


def _sattn_request(qr, kn_new, vn_new, kp, vp, pos0, ts):
    n_pages = len(kp)
    nrow = qr.shape[0]
    ppb = MOBA_BLOCK // PAGE_SIZE
    n_past = n_pages // ppb
    blk_lane = lax.broadcasted_iota(jnp.int32, (BRANCH_W, LANES), 1)
    kmean_t = jnp.zeros((BRANCH_W, LANES), F32)
    for n in range(n_past):
        acc = kp[n * ppb][...]
        for r in range(1, ppb):
            acc = acc + kp[n * ppb + r][...]
        kmean_t = jnp.where(blk_lane == n, jnp.sum(acc, axis=1, keepdims=True) * (1.0 / MOBA_BLOCK), kmean_t)
    gate = jnp.dot(qr, kmean_t, precision=lax.Precision.HIGHEST, preferred_element_type=F32)
    bias = _top_blocks_bias(gate, n_past)
    qb = (qr * (A_DH ** -0.5)).astype(BF)
    scores = []
    for p in range(n_pages):
        s = jnp.dot(qb, kp[p][...].astype(BF), preferred_element_type=F32)
        scores.append(s + bias[:, p // ppb:p // ppb + 1])
    pad = jnp.zeros((LANES - kn_new.shape[0], BRANCH_W), F32)
    kn = jnp.concatenate([kn_new, pad], axis=0).astype(BF)
    vn = jnp.concatenate([vn_new, pad], axis=0).astype(BF)
    s_own = lax.dot_general(qb, kn, NT, preferred_element_type=F32)
    qi = lax.broadcasted_iota(jnp.int32, (nrow, LANES), 0) // A_HEADS
    kj = lax.broadcasted_iota(jnp.int32, (nrow, LANES), 1)
    own_blk = pos0 // MOBA_BLOCK
    ok = (kj < ts) & (own_blk * MOBA_BLOCK + kj <= pos0 + qi)
    s_own = jnp.where(ok, s_own, NEG_INF)
    m = jnp.max(s_own, axis=1, keepdims=True)
    for s in scores:
        m = jnp.maximum(m, jnp.max(s, axis=1, keepdims=True))
    p_own = jnp.exp(s_own - m)
    l = jnp.sum(p_own, axis=1, keepdims=True)
    out = jnp.dot(p_own.astype(BF), vn, preferred_element_type=F32)
    for p in range(n_pages):
        pr = jnp.exp(scores[p] - m)
        l = l + jnp.sum(pr, axis=1, keepdims=True)
        out = out + lax.dot_general(pr.astype(BF), vp[p][...].astype(BF), NT, preferred_element_type=F32)
    out = out / l
    hrow = lax.broadcasted_iota(jnp.int32, (nrow, BRANCH_W), 0) % A_HEADS
    hlane = lax.broadcasted_iota(jnp.int32, (nrow, BRANCH_W), 1) // A_DH
    out = jnp.where(hrow == hlane, out, 0.0)
    return jnp.sum(out.reshape(ts, A_HEADS, BRANCH_W), axis=1)


def _sample_attention(qrows, kn, vn, cache_k, cache_v, page_table, layer, pos0, ts):
    Bd, n_pages = page_table.shape
    nrow = qrows.shape[1]
    rps = REQS_PER_STEP
    assert Bd % rps == 0
    page = lambda r, p: pl.BlockSpec((None, None, BRANCH_W, PAGE_SIZE),
                                     lambda g, pt, r=r, p=p: (layer, pt[(g * rps + r) * n_pages + p], 0, 0))
    req = lambda rows: pl.BlockSpec((rps, rows, BRANCH_W), lambda g, pt: (g, 0, 0))
    pages = [page(r, p) for r in range(rps) for _ in range(2) for p in range(n_pages)]
    caches = [c for _ in range(rps) for c in ([cache_k] * n_pages + [cache_v] * n_pages)]
    grid_spec = pltpu.PrefetchScalarGridSpec(
        num_scalar_prefetch=1,
        grid=(Bd // rps,),
        in_specs=[req(nrow), req(kn.shape[1]), req(vn.shape[1])] + pages,
        out_specs=req(ts),
    )
    return pl.pallas_call(
        functools.partial(_sattn_body, n_pages=n_pages, pos0=pos0, ts=ts),
        grid_spec=grid_spec,
        out_shape=jax.ShapeDtypeStruct((Bd, ts, BRANCH_W), F32),
        compiler_params=_cp(("parallel",)),
        name="moba_sample",
    )(page_table.reshape(-1), qrows, kn, vn, *caches)


def _smix_body(uf_ref, db_ref, dc_ref, dh_ref, zb_ref, pw_ref, ps_ref, sw_ref, yb_ref, yd_ref, zn_ref, *, bd, ts, pos0):
    ps, sw = ps_ref[...], sw_ref[...]
    slab = lambda ref, r: ref[r * bd:(r + 1) * bd, :]
    z = [slab(zb_ref, r) for r in range(SCONV_W - 1)] + [slab(dc_ref, t) * slab(dh_ref, t) for t in range(ts)]
    for t in range(ts):
        u = slab(uf_ref, POOL_BUF + t)
        for g, w in enumerate(POOL_WINDOWS):
            sl = slice(g * POOL_GC, (g + 1) * POOL_GC)
            win = u[:, sl]
            for r in range(1, w):
                win = win + slab(uf_ref, POOL_BUF + t - r)[:, sl]
            pooled = win / float(min(w, pos0 + t + 1)) - u[:, sl]
            y = jnp.dot(pooled.astype(BF), pw_ref[g], preferred_element_type=F32) * ps[:, sl]
            yb_ref[t * bd:(t + 1) * bd, sl] = y.astype(yb_ref.dtype)
        conv = sw[0:1] * z[t] + sw[1:2] * z[t + 1] + sw[2:3] * z[t + 2]
        yd_ref[t * bd:(t + 1) * bd, :] = (slab(db_ref, t) * conv).astype(yd_ref.dtype)
    for r in range(SCONV_W - 1):
        zn_ref[r * bd:(r + 1) * bd, :] = z[ts + r]


def _sample_mixers(ufull, proj, zbuf, pool_w_bf, pool_scale, sconv_w, Bd, Ts, pos0):
    M = Ts * Bd
    whole = lambda a: pl.BlockSpec(a.shape, lambda i, n=a.ndim: (0,) * n)
    blk = lambda cb: pl.BlockSpec((M, BRANCH_W), lambda i, cb=cb: (0, cb))
    out = lambda r, dt: (pl.BlockSpec((r, BRANCH_W), lambda i: (0, 0)), jax.ShapeDtypeStruct((r, BRANCH_W), dt))
    outs = [out(M, BF), out(M, BF), out((SCONV_W - 1) * Bd, F32)]
    return pl.pallas_call(
        functools.partial(_smix_body, bd=Bd, ts=Ts, pos0=pos0),
        grid=(1,),
        in_specs=[whole(ufull), blk(CB_DB), blk(CB_DC), blk(CB_DH), whole(zbuf), whole(pool_w_bf), whole(pool_scale),
                  whole(sconv_w)],
        out_specs=[o[0] for o in outs],
        out_shape=[o[1] for o in outs],
        compiler_params=_cp(("arbitrary",)),
        name="pool_sconv_sample",
    )(ufull, proj, proj, proj, zbuf, pool_w_bf, pool_scale, sconv_w)


def _shgrn_body(cq_ref, cf_ref, ci_ref, cg_ref, s0_ref, lb_ref, ng_ref, yc_ref, so_ref, *, ts):
    lb, ng = lb_ref[...], ng_ref[...]
    for r in range(cq_ref.shape[0]):
        for h in range(C_HEADS):
            sl = slice(h * C_DK, (h + 1) * C_DK)
            o, st = _gla_chunk(cq_ref[r, :, sl], cf_ref[r, :, sl], ci_ref[r, :, sl], lb[:, sl], s0_ref[r, h].T, ts)
            so_ref[r, h] = st.T
            yc_ref[r, :, sl] = _hgrn_out(o, cg_ref[r, :, sl], ng[:, sl])


def _sample_hgrn(cq, cf, ci, cg, states, lb, ng, ts, layer, carried):
    Bd, rows, _ = cq.shape
    depth = states.shape[0]
    rps = math.gcd(Bd, 4)
    req = pl.BlockSpec((rps, rows, BRANCH_W), lambda b: (b, 0, 0))
    st_block = (rps, C_HEADS, C_DK, C_DK)
    st_in = pl.BlockSpec((None,) + st_block, lambda b: (layer, b, 0, 0, 0))
    st_out, st_shape = _layer_slot(layer, depth, states.shape[1:], st_block, lambda b: (b, 0, 0, 0))
    vec = pl.BlockSpec((1, BRANCH_W), lambda b: (0, 0))
    carried = [] if carried is None else [carried]
    n_in = 7
    body = functools.partial(_shgrn_body, ts=ts)
    if carried:
        body = lambda *refs, inner=body: inner(*refs[:n_in], *refs[n_in + 1:])
    return pl.pallas_call(
        body,
        grid=(Bd // rps,),
        in_specs=[req, req, req, req, st_in, vec, vec] + [pl.BlockSpec(memory_space=pl.ANY)] * len(carried),
        out_specs=[req, st_out],
        out_shape=[jax.ShapeDtypeStruct((Bd, rows, BRANCH_W), F32), st_shape],
        input_output_aliases={n_in: 1} if carried else {},
        compiler_params=_cp(("parallel",)),
        name="hgrn_sample",
    )(cq, cf, ci, cg, states, lb, ng, *carried)


def _prompt_layer(x, B, T, w, lb, tables, layer, depth, kv_carried):
    tm = min(T, 1024)
    proj = _matmul(x, w["w_mix"], tm, N_MIX // 2, "proj_prompt")
    gates = _matmul(x, w["w_gate"], tm, N_GATE // 2, "gate_prompt", gate=True)
    kt, vtf, kb, qt, vt, bias = _attn_prep(proj, tables, B, T, layer, depth, kv_carried)
    ya = _prompt_attention(qt, kb, vt, bias, B, T)
    yb, yd, ztail = _prompt_mixers(proj, w["pool_w"], w["pool_scale"], w["sconv_w"], B, T, min(T, 512))
    yc, s_new = _prompt_hgrn(proj, lb, w["hgrn_norm_g"], B, T, min(T, 512))
    h = _merge((ya, yb, yc, yd), gates, x, w["w_branch"], w["w_o"], w["ln1_g"], w["ln1_b"], 512)
    act, up_tail = _prompt_ffn_up(h, w["ffn_up"], w["ffn_conv"], B, T, 256)
    out = _prompt_ffn_down(act, w["ffn_down"], h, w["ln2_g"], w["ln2_b"], 512)
    proj3 = proj.reshape(B, T, N_MIX)
    new_pool = proj3[:, T - POOL_BUF:, CB_PU * BRANCH_W:(CB_PU + 1) * BRANCH_W]
    new_sconv = ztail[:, 8 - (SCONV_W - 1):]
    new_ffn = up_tail[:, 8 - (FFN_CONV_W - 1):]
    return out, (kt, vtf), new_pool, s_new, new_sconv, new_ffn


def _to_tm(a):
    return a.transpose(1, 0, 2).reshape(a.shape[1] * a.shape[0], a.shape[2])


def _from_tm(a, Bd):
    return a.reshape(a.shape[0] // Bd, Bd, a.shape[1]).transpose(1, 0, 2)


def _sample_layer(x, Bd, Ts, pos0, layer, w, lb, tables, cache_k, cache_v, page_table, pool_buf, hgrn_states, hgrn_carried,
                  sconv_buf, ffn_buf):
    M = Ts * Bd
    proj = _matmul(x, w["w_mix"], M, BRANCH_W, "proj_sample")
    gates = _matmul(x, w["w_gate"], M, BRANCH_W, "gate_sample", gate=True)
    qf, _, kf, _, _ = _rope(proj, tables, M)
    col = lambda cb: proj[:, cb * BRANCH_W:(cb + 1) * BRANCH_W]
    q_r, k_r, v_r = _from_tm(qf, Bd), _from_tm(kf, Bd), _from_tm(col(CB_AV), Bd)
    head_of_lane = jnp.arange(BRANCH_W) // A_DH
    hmask = (head_of_lane[None, :] == jnp.arange(A_HEADS)[:, None]).astype(F32)
    qrows = (q_r[:, :, None, :] * hmask[None, None]).reshape(Bd, Ts * A_HEADS, BRANCH_W)
    pad8 = lambda a: jnp.pad(a, ((0, 0), (0, 8 - Ts), (0, 0)))
    ya = _sample_attention(qrows, pad8(k_r), pad8(v_r), cache_k, cache_v, page_table, layer, pos0, Ts)
    ufull = jnp.concatenate([_to_tm(pool_buf), col(CB_PU)], axis=0)
    yb, yd, znew = _sample_mixers(ufull, proj, _to_tm(sconv_buf), w["pool_w"], w["pool_scale"], w["sconv_w"], Bd, Ts, pos0)
    req8 = lambda cb: pad8(_from_tm(col(cb), Bd))
    yc, s_new = _sample_hgrn(req8(CB_CQ), req8(CB_CF), req8(CB_CI), req8(CB_CG), hgrn_states, lb, w["hgrn_norm_g"], Ts,
                             layer, hgrn_carried)
    ys = (_to_tm(ya), yb, _to_tm(yc[:, :Ts]), yd)
    h = _merge(ys, gates, x, w["w_branch"], w["w_o"], w["ln1_g"], w["ln1_b"], min(M, 256))
    up = _matmul(h, w["ffn_up"], M, BRANCH_W, "ffn_up_sample")
    full = jnp.concatenate([_to_tm(ffn_buf), up], axis=0)
    out = _sample_ffn(full, w["ffn_conv"], w["ffn_down"], h, w["ln2_g"], w["ln2_b"], Bd, Ts)
    k = k_r.reshape(Bd, Ts, A_HEADS, A_DH)
    v = v_r.reshape(Bd, Ts, A_HEADS, A_DH)
    new_pool = _from_tm(ufull[Ts * Bd:], Bd)
    new_sconv = _from_tm(znew, Bd)
    new_ffn = _from_tm(full[Ts * Bd:], Bd)
    return out, k, v, new_pool, s_new, new_sconv, new_ffn


def _layer_weights(l, w_in, w_branch, w_o, pool_w, pool_scale, hgrn_norm_g, sconv_w, ln1_g, ln1_b, ffn_up, ffn_conv,
                   ffn_down, ln2_g, ln2_b):
    row = lambda a: a[l][None, :]
    return {
        "w_mix": w_in[l][:, :N_MIX].astype(BF), "w_gate": w_in[l][:, N_MIX:].astype(BF),
        "w_branch": w_branch[l].astype(BF), "w_o": w_o[l].astype(BF),
        "pool_w": pool_w[l].astype(BF), "pool_scale": row(pool_scale), "hgrn_norm_g": row(hgrn_norm_g),
        "sconv_w": sconv_w[l], "ln1_g": row(ln1_g), "ln1_b": row(ln1_b), "ffn_up": ffn_up[l].astype(BF),
        "ffn_conv": ffn_conv[l], "ffn_down": ffn_down[l].astype(BF), "ln2_g": row(ln2_g), "ln2_b": row(ln2_b),
    }


def kernel(x_prompt, x_sample, cache_k, cache_v, state_pool, state_hgrn, state_sconv, state_ffn, page_table, w_in,
           w_branch, w_o, pool_w, pool_scale, hgrn_lb_logits, hgrn_norm_g, sconv_w, ln1_g, ln1_b, ffn_up, ffn_conv,
           ffn_down, ln2_g, ln2_b):
    depth = w_in.shape[0]
    Bp, T, _ = x_prompt.shape
    Bd, Ts, _ = x_sample.shape
    n_pages = page_table.shape[1]
    pos0 = n_pages * PAGE_SIZE
    assert T % MOBA_BLOCK == 0 and pos0 % MOBA_BLOCK == 0 and Ts <= 8 and T >= HALO
    lb_all = _lower_bounds(hgrn_lb_logits.astype(F32))
    tab_p = _rope_tables(jnp.arange(T))
    tab_s = _rope_tables(pos0 + jnp.repeat(jnp.arange(Ts), Bd))
    ck = cache_k.transpose(0, 1, 3, 4, 2).reshape(depth, cache_k.shape[1], BRANCH_W, PAGE_SIZE)
    cv = cache_v.transpose(0, 1, 3, 4, 2).reshape(depth, cache_v.shape[1], BRANCH_W, PAGE_SIZE)
    hp = x_prompt.reshape(Bp * T, D_MODEL)
    hs = _to_tm(x_sample)
    outs_p, outs_s = [], []
    kv_p = hgrn_s = None
    for l in range(depth):
        w = _layer_weights(l, w_in, w_branch, w_o, pool_w, pool_scale, hgrn_norm_g, sconv_w, ln1_g, ln1_b, ffn_up,
                           ffn_conv, ffn_down, ln2_g, ln2_b)
        lb = lb_all[l][None, :]
        rp = _prompt_layer(hp, Bp, T, w, lb, tab_p, l, depth, kv_p)
        rs = _sample_layer(hs, Bd, Ts, pos0, l, w, lb, tab_s, ck, cv, page_table, state_pool[l], state_hgrn, hgrn_s,
                           state_sconv[l], state_ffn[l])
        hp, kv_p = rp[0], rp[1]
        hs, hgrn_s = rs[0], rs[4]
        outs_p.append(rp[2:])
        outs_s.append(rs[1:4] + rs[5:])
    stack = lambda outs: [jnp.stack([o[j] for o in outs]) for j in range(len(outs[0]))]
    tokens_first = lambda a: a.reshape(depth, Bp, A_HEADS, A_DH, T).transpose(0, 1, 4, 2, 3)
    pool_p, hgrn_p, sconv_p, ffn_p = stack(outs_p)
    k_s, v_s, pool_s, sconv_s, ffn_s = stack(outs_s)
    return (hp.reshape(Bp, T, D_MODEL), _from_tm(hs, Bd), tokens_first(kv_p[0]), tokens_first(kv_p[1]), pool_p, hgrn_p,
            sconv_p, ffn_p, k_s, v_s, pool_s, hgrn_s, sconv_s, ffn_s)
```
